```python
import math
import jax
import jax.numpy as jnp
from jax import lax
import numpy as np

D_MODEL = 1024
BATCH = 4
SEQ = 4096
DEPTH = 4
DEC_BATCH = 32
DEC_SEQ = 1
PAST_LEN = 8192
PAGE_SIZE = 128

N_MIXERS = 3
N_A = len(range(0, DEPTH, N_MIXERS))
N_B = len(range(1, DEPTH, N_MIXERS))
N_C = len(range(2, DEPTH, N_MIXERS))

RW_HEAD = 64
RW_HEADS = D_MODEL // RW_HEAD
RW_DECAY_LORA = 64
RW_AAA_LORA = 64
RW_MV_LORA = 32
RW_GATE_LORA = 128
RW_DECAY_SCALE = math.exp(-0.5)
RW_GN_EPS = 64e-5

HG_EXPAND = 128
HG_HEADS = D_MODEL // HG_EXPAND
HG_HEAD_V = D_MODEL // HG_HEADS
HG_CHUNK = 16

FX_HEAD = 64
FX_HEADS = D_MODEL // FX_HEAD
FX_QBLOCK = 128

D_FF = -(-8 * D_MODEL // (3 * 256)) * 256
NORM_EPS = 1e-6
F32 = jnp.float32

kernel_name = 'hybrid_rwkv7_hgrn2_fox_step'


def _rms_norm(x, g, eps=NORM_EPS):
    xf = x.astype(F32)
    y = xf * lax.rsqrt(jnp.mean(xf * xf, axis=-1, keepdims=True) + eps)
    return (y * g.astype(F32)).astype(x.dtype)


def _swiglu(h, wg, wu, wd):
    return (jax.nn.silu(h @ wg) * (h @ wu)) @ wd


def _wkv7_scan(r, w, k, v, kk, a, S0):
    def step(S, inp):
        r_t, w_t, k_t, v_t, kk_t, a_t = inp
        sa = jnp.einsum('bhvk,bhk->bhv', S, kk_t)
        S = (S * w_t[:, :, None, :] - sa[..., None] * (kk_t * a_t)[:, :, None, :]
             + v_t[..., None] * k_t[:, :, None, :])
        return S, jnp.einsum('bhvk,bhk->bhv', S, r_t)
    xs = tuple(jnp.moveaxis(t, 1, 0) for t in (r, w, k, v, kk, a))
    S, o = lax.scan(step, S0, xs)
    return jnp.moveaxis(o, 0, 1), S


def _rwkv7_time_mix(h, x_prev0, S0, v_first, mu, wr, wk, wv, wo, w0, w1, w2, a0, a1, a2,
                    vres, g1, g2, k_k, k_a, r_k, ln_w, ln_b):
    B, T, D = h.shape
    hs = (B, T, RW_HEADS, RW_HEAD)
    prev = jnp.concatenate([x_prev0[:, None, :].astype(h.dtype), h[:, :-1]], axis=1)
    xm = h[:, :, None, :] + (prev - h)[:, :, None, :] * mu.astype(h.dtype)
    xr, xw, xk, xv, xa, xg = (xm[:, :, n] for n in range(6))
    r = xr @ wr
    k = xk @ wk
    v = xv @ wv
    w = jnp.exp(-RW_DECAY_SCALE * jax.nn.sigmoid((w0 + jnp.tanh(xw @ w1) @ w2).astype(F32)))
    a = jax.nn.sigmoid(a0 + (xa @ a1) @ a2)
    if vres is None:
        v_first = v
    else:
        v0, v1, v2 = vres
        v = v + (v_first - v) * jax.nn.sigmoid(v0 + (xv @ v1) @ v2)
    g = jax.nn.sigmoid(xg @ g1) @ g2
    kk = (k * k_k).astype(F32).reshape(hs)
    kk = kk / jnp.maximum(jnp.sqrt(jnp.sum(kk * kk, axis=-1, keepdims=True)), 1e-12)
    k = k * (1.0 + (a - 1.0) * k_a)
    rf, kf, vf, af = (t.astype(F32).reshape(hs) for t in (r, k, v, a))
    o, S = _wkv7_scan(rf, w.reshape(hs), kf, vf, kk, af, S0.astype(F32))
    mu_o = jnp.mean(o, axis=-1, keepdims=True)
    var_o = jnp.mean(jnp.square(o - mu_o), axis=-1, keepdims=True)
    o = ((o - mu_o) * lax.rsqrt(var_o + RW_GN_EPS)).reshape(B, T, D) * ln_w.astype(F32) + ln_b.astype(F32)
    bonus = (jnp.sum(rf * kf * r_k.astype(F32), axis=-1, keepdims=True) * vf).reshape(B, T, D)
    y = ((o + bonus) * g.astype(F32)).astype(h.dtype) @ wo
    return y, S, h[:, -1], v_first


def _gla_chunked(q, k, v, logf, S0):
    B, T, H, K = q.shape
    V = v.shape[-1]
    C = min(HG_CHUNK, T)
    nC = -(-T // C)
    pad = nC * C - T

    def blocks(t):
        t = jnp.pad(t, ((0, 0), (0, pad), (0, 0), (0, 0)))
        return t.reshape(B, nC, C, H, t.shape[-1]).transpose(1, 0, 3, 2, 4)

    mask = jnp.tril(jnp.ones((C, C), dtype=bool))

    def step(S, blk):
        qc, kc, vc, gc = blk
        b = jnp.cumsum(gc, axis=2)
        qd = qc * jnp.exp(b)
        kd = kc * jnp.exp(-b)
        att = jnp.where(mask, jnp.einsum('bhtk,bhsk->bhts', qd, kd), 0.0)
        o = jnp.einsum('bhtk,bhkv->bhtv', qd, S) + jnp.einsum('bhts,bhsv->bhtv', att, vc)
        b_last = b[:, :, -1:, :]
        S = (jnp.exp(b_last[:, :, 0, :])[..., None] * S
             + jnp.einsum('bhck,bhcv->bhkv', kc * jnp.exp(b_last - b), vc))
        return S, o

    S, o = lax.scan(step, S0, tuple(blocks(t) for t in (q, k, v, logf)))
    o = o.transpose(1, 0, 3, 2, 4).reshape(B, nC * C, H, V)[:, :T]
    return o, S


def _hgrn2_mix(h, S0, lb, w_in, norm_g, wo):
    B, T, D = h.shape
    q, fz, i_, g = jnp.split(h @ w_in, 4, axis=-1)
    f = lb + (1.0 - lb) * jax.nn.sigmoid(fz.astype(F32))
    hs = (B, T, HG_HEADS, HG_EXPAND)
    qf = jax.nn.silu(q).astype(F32).reshape(hs)
    kf = (1.0 - f).reshape(hs)
    logf = jnp.log(f).reshape(hs)
    vf = i_.astype(F32).reshape(B, T, HG_HEADS, HG_HEAD_V)
    o, S = _gla_chunked(qf, kf, vf, logf, S0.astype(F32))
    o = _rms_norm(o, norm_g.reshape(HG_HEADS, HG_HEAD_V)).reshape(B, T, D)
    y = (o * jax.nn.silu(g.astype(F32))).astype(h.dtype) @ wo
    return y, S


def _fox_project(h, w_in, b_f, q_norm, k_norm):
    B, T, D = h.shape
    proj = h @ w_in
    q, k, v, g = (proj[..., n * D:(n + 1) * D] for n in range(4))
    logf = jax.nn.log_sigmoid(proj[..., 4 * D:].astype(F32) + b_f.astype(F32))
    hs = (B, T, FX_HEADS, FX_HEAD)
    q = _rms_norm(q.reshape(hs), q_norm)
    k = _rms_norm(k.reshape(hs), k_norm)
    return q, k, v.reshape(hs), g, logf


def _fox_prompt_attention(q, k, v, logf):
    B, T, H, Dh = q.shape
    nQ = T // FX_QBLOCK
    scale = Dh ** -0.5
    c = lax.cumsum(logf, axis=1).transpose(0, 2, 1)
    kh = k.transpose(0, 2, 1, 3)
    vh = v.transpose(0, 2, 1, 3)
    qb = q.reshape(B, nQ, FX_QBLOCK, H, Dh).transpose(1, 0, 3, 2, 4)
    cb = c.reshape(B, H, nQ, FX_QBLOCK).transpose(2, 0, 1, 3)
    kpos = jnp.arange(T)

    def block(args):
        qi, cq, i = args
        s = jnp.einsum('bhqd,bhkd->bhqk', qi, kh).astype(F32) * scale + cq[..., None] - c[:, :, None, :]
        qpos = i * FX_QBLOCK + jnp.arange(FX_QBLOCK)
        s = jnp.where(kpos[None, :] <= qpos[:, None], s, -jnp.inf)
        p = jax.nn.softmax(s, axis=-1)
        return jnp.einsum('bhqk,bhkd->bhqd', p.astype(vh.dtype), vh)

    o = lax.map(block, (qb, cb, jnp.arange(nQ)))
    return o.transpose(1, 0, 3, 2, 4).reshape(B, T, H, Dh)


def _fox_paged_attention(q, k, v, logf, cache_k, cache_v, cache_logf, page_table, j):
    B, T, H, Dh = q.shape
    scale = Dh ** -0.5
    pk = cache_k[j][page_table].reshape(B, -1, H, Dh).astype(q.dtype)
    pv = cache_v[j][page_table].reshape(B, -1, H, Dh).astype(v.dtype)
    plf = cache_logf[j][page_table].reshape(B, -1, H).astype(F32)
    P = pk.shape[1]
    suffix = (lax.cumsum(plf, axis=1, reverse=True) - plf).transpose(0, 2, 1)
    cq = lax.cumsum(logf, axis=1).transpose(0, 2, 1)
    s_past = (jnp.einsum('bqhd,bkhd->bhqk', q, pk).astype(F32) * scale
              + cq[..., None] + suffix[:, :, None, :])
    s_new = jnp.einsum('bqhd,bkhd->bhqk', q, k).astype(F32) * scale + cq[..., None] - cq[:, :, None, :]
    s_new = jnp.where(jnp.tril(jnp.ones((T, T), dtype=bool)), s_new, -jnp.inf)
    p = jax.nn.softmax(jnp.concatenate([s_past, s_new], axis=-1), axis=-1).astype(v.dtype)
    return (jnp.einsum('bhqk,bkhd->bqhd', p[..., :P], pv)
            + jnp.einsum('bhqk,bkhd->bqhd', p[..., P:], v))


def setup_inputs(seed: int = 0) -> dict:
    key = jax.random.key(seed)
    keys = jax.random.split(key, 64)
    cnt = [0]

    def nk():
        cnt[0] += 1
        return keys[cnt[0] - 1]

    def nrm(shape, scale=1.0, shift=0.0):
        return shift + scale * jax.random.normal(nk(), shape, F32)

    D, H, N = D_MODEL, RW_HEADS, RW_HEAD
    n_pages = PAST_LEN // PAGE_SIZE
    n_used = DEC_BATCH * n_pages
    n_phys = n_used + max(1, n_used // 4)
    inp = {}
    inp['x_prompt'] = nrm((BATCH, SEQ, D))
    inp['x_sample'] = nrm((DEC_BATCH, DEC_SEQ, D))
    inp['state_rwkv_wkv'] = nrm((N_A, DEC_BATCH, H, N, N), 0.3)
    inp['state_rwkv_shift'] = nrm((N_A, DEC_BATCH, D))
    inp['state_hgrn'] = nrm((N_B, DEC_BATCH, HG_HEADS, HG_EXPAND, HG_HEAD_V), 0.3)
    inp['cache_fox_k'] = nrm((N_C, n_phys, PAGE_SIZE, FX_HEADS, FX_HEAD))
    inp['cache_fox_v'] = nrm((N_C, n_phys, PAGE_SIZE, FX_HEADS, FX_HEAD))
    inp['cache_fox_logf'] = jax.nn.log_sigmoid(nrm((N_C, n_phys, PAGE_SIZE, FX_HEADS), 1.0, 9.0))
    perm = jax.random.permutation(nk(), n_phys)
    inp['page_table'] = perm[:n_used].reshape(DEC_BATCH, n_pages).astype(jnp.int32)
    inp['norm_pre_mix'] = nrm((DEPTH, D), 0.1, 1.0)
    inp['norm_post_mix'] = nrm((DEPTH, D), 0.1, 1.0)
    inp['norm_pre_ffn'] = nrm((DEPTH, D), 0.1, 1.0)
    inp['norm_post_ffn'] = nrm((DEPTH, D), 0.1, 1.0)
    inp['rw_mu'] = jax.random.uniform(nk(), (N_A, 6, D), F32)
    inp['rw_wr'] = nrm((N_A, D, D), D ** -0.5)
    inp['rw_wk'] = nrm((N_A, D, D), D ** -0.5)
    inp['rw_wv'] = nrm((N_A, D, D), D ** -0.5)
    inp['rw_wo'] = nrm((N_A, D, D), D ** -0.5)
    inp['rw_w0'] = nrm((N_A, D), 0.5)
    inp['rw_w1'] = nrm((N_A, D, RW_DECAY_LORA), D ** -0.5)
    inp['rw_w2'] = nrm((N_A, RW_DECAY_LORA, D), 0.5 * RW_DECAY_LORA ** -0.5)
    inp['rw_a0'] = nrm((N_A, D), 0.1)
    inp['rw_a1'] = nrm((N_A, D, RW_AAA_LORA), D ** -0.5)
    inp['rw_a2'] = nrm((N_A, RW_AAA_LORA, D), 0.5 * RW_AAA_LORA ** -0.5)
    inp['rw_v0'] = nrm((N_A - 1, D), 0.1)
    inp['rw_v1'] = nrm((N_A - 1, D, RW_MV_LORA), D ** -0.5)
    inp['rw_v2'] = nrm((N_A - 1, RW_MV_LORA, D), 0.5 * RW_MV_LORA ** -0.5)
    inp['rw_g1'] = nrm((N_A, D, RW_GATE_LORA), D ** -0.5)
    inp['rw_g2'] = nrm((N_A, RW_GATE_LORA, D), RW_GATE_LORA ** -0.5)
    inp['rw_kk'] = nrm((N_A, D), 0.1, 1.0)
    inp['rw_ka'] = nrm((N_A, D), 0.1, 1.0)
    inp['rw_rk'] = nrm((N_A, H, N), 0.1)
    inp['rw_lnw'] = nrm((N_A, D), 0.1, 1.0)
    inp['rw_lnb'] = nrm((N_A, D), 0.01)
    inp['hg_w_in'] = nrm((N_B, D, 4 * D), D ** -0.5)
    inp['hg_lb'] = nrm((DEPTH, D), 0.5)
    inp['hg_norm'] = nrm((N_B, D), 0.1, 1.0)
    inp['hg_wo'] = nrm((N_B, D, D), D ** -0.5)
    inp['fx_w_in'] = nrm((N_C, D, 4 * D + FX_HEADS), D ** -0.5)
    inp['fx_bf'] = nrm((N_C, FX_HEADS), 2.0, 6.0)
    inp['fx_qn'] = nrm((N_C, FX_HEAD), 0.1, 1.0)
    inp['fx_kn'] = nrm((N_C, FX_HEAD), 0.1, 1.0)
    inp['fx_wo'] = nrm((N_C, D, D), D ** -0.5)
    inp['ffn_wg'] = nrm((DEPTH, D, D_FF), D ** -0.5)
    inp['ffn_wu'] = nrm((DEPTH, D, D_FF), D ** -0.5)
    inp['ffn_wd'] = nrm((DEPTH, D_FF, D), D_FF ** -0.5)
    return inp


def reference(x_prompt, x_sample, state_rwkv_wkv, state_rwkv_shift, state_hgrn,
              cache_fox_k, cache_fox_v, cache_fox_logf, page_table,
              norm_pre_mix, norm_post_mix, norm_pre_ffn, norm_post_ffn,
              rw_mu, rw_wr, rw_wk, rw_wv, rw_wo, rw_w0, rw_w1, rw_w2, rw_a0, rw_a1, rw_a2,
              rw_v0, rw_v1, rw_v2, rw_g1, rw_g2, rw_kk, rw_ka, rw_rk, rw_lnw, rw_lnb,
              hg_w_in, hg_lb, hg_norm, hg_wo,
              fx_w_in, fx_bf, fx_qn, fx_kn, fx_wo,
              ffn_wg, ffn_wu, ffn_wd):
    lb_sm = jax.nn.softmax(hg_lb.astype(F32), axis=0)
    lb_layers = jnp.cumsum(lb_sm, axis=0) - lb_sm[0]

    def trunk(x, wkv0, shift0, hg0, attend):
        v_first = None
        wkv_n, shift_n, hg_n, k_n, v_n, lf_n = [], [], [], [], [], []
        for i in range(DEPTH):
            j = i // N_MIXERS
            h = _rms_norm(x, norm_pre_mix[i])
            if i % N_MIXERS == 0:
                vres = None if j == 0 else (rw_v0[j - 1], rw_v1[j - 1], rw_v2[j - 1])
                o, S, last, v_first = _rwkv7_time_mix(
                    h, shift0[j], wkv0[j], v_first, rw_mu[j], rw_wr[j], rw_wk[j], rw_wv[j], rw_wo[j],
                    rw_w0[j], rw_w1[j], rw_w2[j], rw_a0[j], rw_a1[j], rw_a2[j], vres,
                    rw_g1[j], rw_g2[j], rw_kk[j], rw_ka[j], rw_rk[j], rw_lnw[j], rw_lnb[j])
                wkv_n.append(S)
                shift_n.append(last)
            elif i % N_MIXERS == 1:
                o, S = _hgrn2_mix(h, hg0[j], lb_layers[i], hg_w_in[j], hg_norm[j], hg_wo[j])
                hg_n.append(S)
            else:
                q, k, v, g, logf = _fox_project(h, fx_w_in[j], fx_bf[j], fx_qn[j], fx_kn[j])
                att = attend(q, k, v, logf, j)
                B, T = h.shape[:2]
                o = (att.reshape(B, T, D_MODEL) * jax.nn.sigmoid(g)) @ fx_wo[j]
                k_n.append(k)
                v_n.append(v)
                lf_n.append(logf)
            x = x + _rms_norm(o, norm_post_mix[i])
            h = _rms_norm(x, norm_pre_ffn[i])
            x = x + _rms_norm(_swiglu(h, ffn_wg[i], ffn_wu[i], ffn_wd[i]), norm_post_ffn[i])
        return (x, jnp.stack(wkv_n), jnp.stack(shift_n), jnp.stack(hg_n),
                jnp.stack(k_n), jnp.stack(v_n), jnp.stack(lf_n))

    bp = x_prompt.shape[0]
    wkv_zero = jnp.zeros((N_A, bp, RW_HEADS, RW_HEAD, RW_HEAD), F32)
    shift_zero = jnp.zeros((N_A, bp, D_MODEL), x_prompt.dtype)
    hg_zero = jnp.zeros((N_B, bp, HG_HEADS, HG_EXPAND, HG_HEAD_V), F32)
    y_prompt, wkv_p, shift_p, hg_p, k_p, v_p, lf_p = trunk(
        x_prompt, wkv_zero, shift_zero, hg_zero,
        lambda q, k, v, lf, j: _fox_prompt_attention(q, k, v, lf))
    y_sample, wkv_s, shift_s, hg_s, k_s, v_s, lf_s = trunk(
        x_sample, state_rwkv_wkv, state_rwkv_shift, state_hgrn,
        lambda q, k, v, lf, j: _fox_paged_attention(q, k, v, lf, cache_fox_k, cache_fox_v,
                                                    cache_fox_logf, page_table, j))
    return (y_prompt, y_sample, wkv_p, wkv_s, shift_p, shift_s, hg_p, hg_s,
            k_p, k_s, v_p, v_s, lf_p, lf_s)
```

```python
import functools

import jax
import jax.numpy as jnp
from jax import lax
from jax.experimental import pallas as pl
from jax.experimental.pallas import tpu as pltpu

F32 = jnp.float32
BF16 = jnp.bfloat16

LANES = 128
VMEM_LIMIT = 56 * 1024 * 1024

D_MODEL = 1024
DEPTH = 4
N_MIXERS = 3
RW_HEAD = 64
RW_HEADS = D_MODEL // RW_HEAD
RW_DECAY_SCALE = 0.6065306597126334
RW_GN_EPS = 64e-5
HG_HEADS = 8
HG_HEAD = D_MODEL // HG_HEADS
HG_CHUNK = 16
FX_HEAD = 64
FX_HEADS = D_MODEL // FX_HEAD
PAGE_SIZE = 128
D_FF = 2816
NORM_EPS = 1e-6
NEG_BIG = -1e30


def _params(sem):
    return pltpu.CompilerParams(dimension_semantics=sem, vmem_limit_bytes=VMEM_LIMIT)


def _const_spec(shape):
    n = len(shape)
    return pl.BlockSpec(shape, lambda *_: (0,) * n, pipeline_mode=pl.Buffered(1))


def _rms(x, g, eps=NORM_EPS):
    return x * lax.rsqrt(jnp.mean(x * x, axis=-1, keepdims=True) + eps) * g


def _sigmoid(x):
    return 1.0 / (1.0 + jnp.exp(-x))


def _bdot(a, b):
    return jnp.dot(a.astype(BF16), b.astype(BF16), preferred_element_type=F32)


def _dot_nt(a, b):
    return lax.dot_general(a.astype(BF16), b.astype(BF16), (((1,), (1,)), ((), ())),
                           preferred_element_type=F32)


def _dot_tn(a, b):
    return lax.dot_general(a.astype(BF16), b.astype(BF16), (((0,), (0,)), ((), ())),
                           preferred_element_type=F32)


def _split(x):
    hi = x.astype(BF16)
    lo = (x - hi.astype(F32)).astype(BF16)
    return hi, lo


def _sel_dot(x, sel):
    hi, lo = _split(x)
    return (jnp.dot(hi, sel, preferred_element_type=F32)
            + jnp.dot(lo, sel, preferred_element_type=F32))


def _sel_dot_left(sel, x):
    hi, lo = _split(x)
    return (jnp.dot(sel, hi, preferred_element_type=F32)
            + jnp.dot(sel, lo, preferred_element_type=F32))


def _iota(shape, dim):
    return lax.broadcasted_iota(jnp.int32, shape, dim)


def _block_ones(n, seg):
    same = (_iota((n, n), 0) // seg) == (_iota((n, n), 1) // seg)
    return jnp.where(same, 1.0, 0.0).astype(BF16)


def _seg_sum(x, seg):
    sel = _block_ones(LANES, seg)
    parts = [_sel_dot(x[:, j * LANES:(j + 1) * LANES], sel) for j in range(x.shape[1] // LANES)]
    return jnp.concatenate(parts, axis=1)


def _rwkv_proj_kernel(*refs, seq_tiles, has_vres, single_step):
    it = iter(refs)
    x_ref, shift_ref, gpre_ref, mu_ref, vec_ref = (next(it) for _ in range(5))
    wr_ref, wk_ref, wv_ref, w1_ref, w2_ref, a1_ref, a2_ref, g1_ref, g2_ref = (next(it) for _ in range(9))
    if has_vres:
        v1_ref, v2_ref, vfirst_ref = (next(it) for _ in range(3))
    r_out, w_out, k_out, v_out, kk_out, b_out, g_out, shift_out = (next(it) for _ in range(8))
    last_scr = next(it)

    h = _rms(x_ref[...], gpre_ref[...])
    tm = h.shape[0]
    if single_step:
        prev = shift_ref[...]
        shift_out[...] = h
    else:
        i = pl.program_id(0)
        b = i // seq_tiles
        ti = i % seq_tiles
        first = jnp.where(ti == 0, shift_ref[pl.ds(b, 1), :], last_scr[...])
        prev = jnp.where(_iota((tm, 1), 0) == 0, first, pltpu.roll(h, 1, 0))
        last_scr[...] = h[tm - 1:tm, :]

        @pl.when(ti == seq_tiles - 1)
        def _():
            shift_out[pl.ds(b, 1), :] = h[tm - 1:tm, :]

    d = prev - h
    mix = lambda n: h + d * mu_ref[n:n + 1, :]
    w0, a0, k_k, k_a = (vec_ref[n:n + 1, :] for n in range(4))

    xv = mix(3)
    r = _bdot(mix(0), wr_ref[...])
    k = _bdot(mix(2), wk_ref[...])
    v = _bdot(xv, wv_ref[...])
    wl = w0 + _bdot(jnp.tanh(_bdot(mix(1), w1_ref[...])), w2_ref[...])
    a = _sigmoid(a0 + _bdot(_bdot(mix(4), a1_ref[...]), a2_ref[...]))
    if has_vres:
        gate = _sigmoid(vec_ref[4:5, :] + _bdot(_bdot(xv, v1_ref[...]), v2_ref[...]))
        v = v + (vfirst_ref[...] - v) * gate
    g = _bdot(_sigmoid(_bdot(mix(5), g1_ref[...])), g2_ref[...])

    kk = k * k_k
    kk = kk / jnp.maximum(jnp.sqrt(_seg_sum(kk * kk, RW_HEAD)), 1e-12)
    r_out[...] = r
    w_out[...] = jnp.exp(-RW_DECAY_SCALE * _sigmoid(wl))
    k_out[...] = k * (1.0 + (a - 1.0) * k_a)
    v_out[...] = v
    kk_out[...] = kk
    b_out[...] = kk * a
    g_out[...] = g


def _rwkv_proj(x, shift0, gpre, mu, vecs, mats, vres, *, seq_len, tm):
    M, D = x.shape
    B = M // seq_len
    single_step = seq_len == 1
    seq_tiles = 1 if single_step else seq_len // tm
    row = pl.BlockSpec((tm, D), lambda i: (i, 0))
    shift_spec = row if single_step else _const_spec((B, D))
    shift_out_spec = row if single_step else pl.BlockSpec((B, D), lambda i: (0, 0))
    ins = [x, shift0, gpre, mu, vecs] + list(mats)
    specs = [row, shift_spec, _const_spec(gpre.shape), _const_spec(mu.shape), _const_spec(vecs.shape)]
    specs += [_const_spec(m.shape) for m in mats]
    if vres is not None:
        v1, v2, vfirst = vres
        ins += [v1, v2, vfirst]
        specs += [_const_spec(v1.shape), _const_spec(v2.shape), row]
    outs = [jax.ShapeDtypeStruct((M, D), F32)] * 7 + [jax.ShapeDtypeStruct((B, D), F32)]
    return pl.pallas_call(
        functools.partial(_rwkv_proj_kernel, seq_tiles=seq_tiles, has_vres=vres is not None,
                          single_step=single_step),
        grid=(M // tm,), in_specs=specs, out_specs=[row] * 7 + [shift_out_spec], out_shape=outs,
        scratch_shapes=[pltpu.VMEM((1, D), F32)],
        compiler_params=_params(("arbitrary",)), name="rwkv_proj")(*ins)


def _wkv_kernel(r_ref, w_ref, k_ref, v_ref, kk_ref, b_ref, s0_ref, o_ref, sout_ref, s_scr, *, tblk, pairs):
    ti = pl.program_id(1)
    n = RW_HEAD

    @pl.when(ti == 0)
    def _():
        for p in range(pairs):
            s_scr[p] = jnp.concatenate([s0_ref[2 * p], s0_ref[2 * p + 1]], axis=1)

    ones2 = _block_ones(LANES, n)
    diag2 = jnp.where(_iota((n, LANES), 0) == (_iota((n, LANES), 1) % n), 1.0, 0.0)

    sub = min(8, tblk)
    single = tblk == sub

    def steps(i, carry):
        t0 = 0 if single else pl.multiple_of(i * sub, sub)
        for p in range(pairs):
            sl = slice(p * LANES, (p + 1) * LANES)
            tr, tw, tk, tv, tkk, tb = (ref[pl.ds(t0, sub), sl] for ref in (r_ref, w_ref, k_ref, v_ref, kk_ref, b_ref))
            s = s_scr[p]
            rows = []
            for u in range(sub):
                row = lambda tile: tile[u:u + 1, :]
                lhs = jnp.concatenate([s * row(tkk), diag2 * row(tv)], axis=0)
                res = jnp.dot(lhs.astype(BF16), ones2, preferred_element_type=F32)
                s = s * row(tw) - res[:n] * row(tb) + res[n:] * row(tk)
                ob = jnp.dot((s * row(tr)).astype(BF16), ones2, preferred_element_type=F32)
                rows.append(jnp.sum(ob * diag2, axis=0, keepdims=True))
            s_scr[p] = s
            o_ref[pl.ds(t0, sub), sl] = jnp.concatenate(rows, axis=0)
        return carry

    if single:
        steps(0, 0)
    else:
        lax.fori_loop(0, tblk // sub, steps, 0)

    @pl.when(ti == pl.num_programs(1) - 1)
    def _():
        for p in range(pairs):
            s = s_scr[p]
            sout_ref[2 * p] = s[:, :n]
            sout_ref[2 * p + 1] = s[:, n:]


def _wkv(r, w, k, v, kk, b, s0, *, tblk):
    B, T, D = r.shape
    H = s0.shape[1]
    seq = pl.BlockSpec((None, tblk, D), lambda bi, ti: (bi, ti, 0))
    st = pl.BlockSpec((None, H, RW_HEAD, RW_HEAD), lambda bi, ti: (bi, 0, 0, 0))
    return pl.pallas_call(
        functools.partial(_wkv_kernel, tblk=tblk, pairs=H // 2),
        grid=(B, T // tblk), in_specs=[seq] * 6 + [st], out_specs=[seq, st],
        out_shape=[jax.ShapeDtypeStruct((B, T, D), F32), jax.ShapeDtypeStruct(s0.shape, F32)],
        scratch_shapes=[pltpu.VMEM((H // 2, RW_HEAD, LANES), F32)],
        compiler_params=_params(("arbitrary", "arbitrary")), name="wkv_scan")(r, w, k, v, kk, b, s0)


def _finish(y, wo_ref, x_ref, gpost_ref, out_ref):
    z = jnp.dot(y.astype(BF16), wo_ref[...], preferred_element_type=F32)
    out_ref[...] = x_ref[...] + _rms(z, gpost_ref[...])


def _rwkv_out_kernel(o_ref, r_ref, k_ref, v_ref, g_ref, x_ref, vec_ref, wo_ref, gpost_ref, out_ref):
    rk, lnw, lnb = (vec_ref[n:n + 1, :] for n in range(3))
    o = o_ref[...]
    dev = o - _seg_sum(o, RW_HEAD) * (1.0 / RW_HEAD)
    var = _seg_sum(dev * dev, RW_HEAD) * (1.0 / RW_HEAD)
    on = dev * lax.rsqrt(var + RW_GN_EPS) * lnw + lnb
    bonus = _seg_sum(r_ref[...] * k_ref[...] * rk, RW_HEAD) * v_ref[...]
    _finish((on + bonus) * g_ref[...], wo_ref, x_ref, gpost_ref, out_ref)


def _hgrn_out_kernel(o_ref, g_ref, x_ref, ng_ref, wo_ref, gpost_ref, out_ref):
    o = o_ref[...]
    parts = []
    for j in range(o.shape[1] // HG_HEAD):
        oj = o[:, j * HG_HEAD:(j + 1) * HG_HEAD]
        parts.append(oj * lax.rsqrt(jnp.mean(oj * oj, axis=-1, keepdims=True) + NORM_EPS))
    g = g_ref[...]
    y = jnp.concatenate(parts, axis=1) * ng_ref[...] * (g * _sigmoid(g))
    _finish(y, wo_ref, x_ref, gpost_ref, out_ref)


def _fox_out_kernel(o_ref, g_ref, x_ref, wo_ref, gpost_ref, out_ref):
    _finish(o_ref[...] * _sigmoid(g_ref[...]), wo_ref, x_ref, gpost_ref, out_ref)


def _mixer_out(body, rows, consts, *, tm, name):
    M, D = rows[0].shape
    row = pl.BlockSpec((tm, D), lambda i: (i, 0))
    return pl.pallas_call(
        body, grid=(M // tm,), in_specs=[row] * len(rows) + [_const_spec(c.shape) for c in consts],
        out_specs=row, out_shape=jax.ShapeDtypeStruct((M, D), F32),
        compiler_params=_params(("arbitrary",)), name=name)(*rows, *consts)


def _ffn_kernel(x_ref, gpre_ref, gpost_ref, wg_ref, wu_ref, wd_ref, out_ref):
    x = x_ref[...]
    h = _rms(x, gpre_ref[...]).astype(BF16)
    a = jnp.dot(h, wg_ref[...], preferred_element_type=F32)
    u = jnp.dot(h, wu_ref[...], preferred_element_type=F32)
    act = (a * _sigmoid(a) * u).astype(BF16)
    z = jnp.dot(act, wd_ref[...], preferred_element_type=F32)
    out_ref[...] = x + _rms(z, gpost_ref[...])


def _ffn(x, gpre, gpost, wg, wu, wd, *, tm):
    M, D = x.shape
    row = pl.BlockSpec((tm, D), lambda i: (i, 0))
    consts = [gpre, gpost, wg, wu, wd]
    return pl.pallas_call(
        _ffn_kernel, grid=(M // tm,), in_specs=[row] + [_const_spec(c.shape) for c in consts],
        out_specs=row, out_shape=jax.ShapeDtypeStruct((M, D), F32),
        compiler_params=_params(("arbitrary",)), name="ffn")(x, *consts)


def _hgrn_proj_kernel(x_ref, gpre_ref, lb_ref, w_ref, q_out, k_out, v_out, lf_out, g_out, *, layer):
    D = x_ref.shape[1]
    h = _rms(x_ref[...], gpre_ref[...]).astype(BF16)
    proj = lambda n: jnp.dot(h, w_ref[:, n * D:(n + 1) * D], preferred_element_type=F32)
    lbw = lb_ref[...]
    e = jnp.exp(lbw - jnp.max(lbw, axis=0, keepdims=True))
    lb = jnp.sum(e[1:layer + 1], axis=0, keepdims=True) / jnp.sum(e, axis=0, keepdims=True)
    q = proj(0)
    f = lb + (1.0 - lb) * _sigmoid(proj(1))
    q_out[...] = q * _sigmoid(q)
    k_out[...] = 1.0 - f
    lf_out[...] = jnp.log(f)
    v_out[...] = proj(2)
    g_out[...] = proj(3)


def _hgrn_proj(x, gpre, hg_lb, w_in, *, layer, tm):
    M, D = x.shape
    row = pl.BlockSpec((tm, D), lambda i: (i, 0))
    return pl.pallas_call(
        functools.partial(_hgrn_proj_kernel, layer=layer), grid=(M // tm,),
        in_specs=[row, _const_spec(gpre.shape), _const_spec(hg_lb.shape), _const_spec(w_in.shape)],
        out_specs=[row] * 5, out_shape=[jax.ShapeDtypeStruct((M, D), F32)] * 5,
        compiler_params=_params(("arbitrary",)), name="hgrn_proj")(x, gpre, hg_lb, w_in)


def _gla_kernel(q_ref, k_ref, v_ref, lf_ref, s0_ref, o_ref, sout_ref, s_scr, *, tblk, heads):
    ti = pl.program_id(1)
    C = HG_CHUNK

    @pl.when(ti == 0)
    def _():
        s_scr[...] = s0_ref[...]

    lower = _iota((C, C), 0) >= _iota((C, C), 1)
    tri = jnp.where(lower, 1.0, 0.0).astype(BF16)

    def chunk(c, carry):
        t0 = pl.multiple_of(c * C, C)
        for hd in range(heads):
            sl = slice(hd * HG_HEAD, (hd + 1) * HG_HEAD)
            blk = lambda ref: ref[pl.ds(t0, C), sl]
            q, k, v = blk(q_ref), blk(k_ref), blk(v_ref)
            b = _sel_dot_left(tri, blk(lf_ref))
            qd = q * jnp.exp(b)
            att = jnp.where(lower, _dot_nt(qd, k * jnp.exp(-b)), 0.0)
            st = s_scr[hd]
            o_ref[pl.ds(t0, C), sl] = _dot_nt(qd, st) + _bdot(att, v)
            b_last = b[C - 1:C, :]
            s_scr[hd] = st * jnp.exp(b_last) + _dot_tn(v, k * jnp.exp(b_last - b))
        return carry

    lax.fori_loop(0, tblk // C, chunk, 0)

    @pl.when(ti == pl.num_programs(1) - 1)
    def _():
        sout_ref[...] = s_scr[...]


def _gla(q, k, v, lf, s0t, *, tblk):
    B, T, D = q.shape
    H = s0t.shape[1]
    seq = pl.BlockSpec((None, tblk, D), lambda bi, ti: (bi, ti, 0))
    st = pl.BlockSpec((None, H, HG_HEAD, HG_HEAD), lambda bi, ti: (bi, 0, 0, 0))
    return pl.pallas_call(
        functools.partial(_gla_kernel, tblk=tblk, heads=H),
        grid=(B, T // tblk), in_specs=[seq] * 4 + [st], out_specs=[seq, st],
        out_shape=[jax.ShapeDtypeStruct((B, T, D), F32), jax.ShapeDtypeStruct(s0t.shape, F32)],
        scratch_shapes=[pltpu.VMEM((H, HG_HEAD, HG_HEAD), F32)],
        compiler_params=_params(("arbitrary", "arbitrary")), name="gla_chunked")(q, k, v, lf, s0t)


def _gla_step_kernel(q_ref, k_ref, v_ref, lf_ref, s0_ref, o_ref, sout_ref, *, heads):
    n = HG_HEAD
    eye = jnp.where(_iota((n, n), 0) == _iota((n, n), 1), 1.0, 0.0)
    ones = jnp.ones((n, n), BF16)
    for hd in range(heads):
        sl = slice(hd * n, (hd + 1) * n)
        vcol = _sel_dot(eye * v_ref[:, sl], ones)
        st = s0_ref[hd] * jnp.exp(lf_ref[:, sl]) + vcol * k_ref[:, sl]
        sout_ref[hd] = st
        o_ref[:, sl] = _dot_nt(jnp.broadcast_to(q_ref[:, sl], (8, n)), st)[0:1, :]


def _gla_step(q, k, v, lf, s0t):
    B, _, D = q.shape
    H = s0t.shape[1]
    seq = pl.BlockSpec((None, 1, D), lambda bi: (bi, 0, 0))
    st = pl.BlockSpec((None, H, HG_HEAD, HG_HEAD), lambda bi: (bi, 0, 0, 0))
    return pl.pallas_call(
        functools.partial(_gla_step_kernel, heads=H), grid=(B,),
        in_specs=[seq] * 4 + [st], out_specs=[seq, st],
        out_shape=[jax.ShapeDtypeStruct((B, 1, D), F32), jax.ShapeDtypeStruct(s0t.shape, F32)],
        compiler_params=_params(("arbitrary",)), name="gla_step")(q, k, v, lf, s0t)


def _fox_proj_kernel(x_ref, gpre_ref, w_ref, wf_ref, bf_ref, qn_ref, kn_ref,
                     q_out, k_out, v_out, g_out, lf_out, c_out, carry_scr, *, seq_tiles, single_step):
    D = x_ref.shape[1]
    tm = x_ref.shape[0]
    h = _rms(x_ref[...], gpre_ref[...]).astype(BF16)
    proj = lambda n: jnp.dot(h, w_ref[:, n * D:(n + 1) * D], preferred_element_type=F32)
    head_norm = lambda t, gain: t * lax.rsqrt(_seg_sum(t * t, FX_HEAD) * (1.0 / FX_HEAD) + NORM_EPS) * gain
    q_out[...] = head_norm(proj(0), qn_ref[...]) * (FX_HEAD ** -0.5)
    k_out[...] = head_norm(proj(1), kn_ref[...])
    v_out[...] = proj(2)
    g_out[...] = proj(3)
    z = jnp.dot(h, wf_ref[...], preferred_element_type=F32) + bf_ref[...]
    lf = jnp.minimum(z, 0.0) - jnp.log1p(jnp.exp(-jnp.abs(z)))
    lf_out[...] = lf[:, :FX_HEADS]
    if single_step:
        c_out[...] = lf[:, :FX_HEADS]
    else:
        ti = pl.program_id(0) % seq_tiles

        @pl.when(ti == 0)
        def _():
            carry_scr[...] = jnp.zeros_like(carry_scr)

        tri = jnp.where(_iota((tm, tm), 0) >= _iota((tm, tm), 1), 1.0, 0.0).astype(BF16)
        c = _sel_dot_left(tri, lf) + carry_scr[...]
        carry_scr[...] = c[tm - 1:tm, :]
        c_out[...] = c[:, :FX_HEADS]


def _fox_proj(x, gpre, w_main, w_f, b_f, qn, kn, *, seq_len, tm):
    M, D = x.shape
    single_step = seq_len == 1
    row = pl.BlockSpec((tm, D), lambda i: (i, 0))
    hrow = pl.BlockSpec((tm, FX_HEADS), lambda i: (i, 0))
    consts = [gpre, w_main, w_f, b_f, qn, kn]
    return pl.pallas_call(
        functools.partial(_fox_proj_kernel, seq_tiles=1 if single_step else seq_len // tm,
                          single_step=single_step),
        grid=(M // tm,), in_specs=[row] + [_const_spec(c.shape) for c in consts],
        out_specs=[row] * 4 + [hrow] * 2,
        out_shape=[jax.ShapeDtypeStruct((M, D), F32)] * 4 + [jax.ShapeDtypeStruct((M, FX_HEADS), F32)] * 2,
        scratch_shapes=[pltpu.VMEM((1, LANES), F32)],
        compiler_params=_params(("arbitrary",)), name="fox_proj")(x, *consts)


def _fox_attn_kernel(q_ref, k_ref, v_ref, cq_ref, ck_ref, o_ref, *, tq):
    qi = pl.program_id(2)
    first = _iota((1, LANES), 1) < FX_HEAD
    q = q_ref[...]
    q0 = jnp.where(first, q, 0.0).astype(BF16)
    q1 = jnp.where(first, 0.0, q).astype(BF16)
    cq = cq_ref[...]
    cq0, cq1 = cq[:, 0:1], cq[:, 1:2]

    def block(ki, carry, diagonal):
        m0, l0, m1, l1, acc = carry
        k0 = pl.multiple_of(ki * tq, tq)
        kb = k_ref[pl.ds(k0, tq), :].astype(BF16)
        vb = v_ref[pl.ds(k0, tq), :]
        ck = ck_ref[ki]
        s0 = _dot_nt(q0, kb) + cq0 - ck[0:1, :]
        s1 = _dot_nt(q1, kb) + cq1 - ck[1:2, :]
        if diagonal:
            keep = _iota((tq, tq), 1) <= _iota((tq, tq), 0)
            s0 = jnp.where(keep, s0, NEG_BIG)
            s1 = jnp.where(keep, s1, NEG_BIG)
        n0 = jnp.maximum(m0, jnp.max(s0, axis=-1, keepdims=True))
        n1 = jnp.maximum(m1, jnp.max(s1, axis=-1, keepdims=True))
        p0 = jnp.exp(s0 - n0)
        p1 = jnp.exp(s1 - n1)
        a0 = jnp.exp(m0 - n0)
        a1 = jnp.exp(m1 - n1)
        l0 = l0 * a0 + jnp.sum(p0, axis=-1, keepdims=True)
        l1 = l1 * a1 + jnp.sum(p1, axis=-1, keepdims=True)
        pv = (_bdot(p0, jnp.where(first, vb, 0.0)) + _bdot(p1, jnp.where(first, 0.0, vb)))
        acc = acc * jnp.where(first, a0, a1) + pv
        return n0, l0, n1, l1, acc

    col = lambda val: jnp.full((tq, 1), val, F32)
    init = (col(NEG_BIG), col(0.0), col(NEG_BIG), col(0.0), jnp.zeros((tq, LANES), F32))
    carry = lax.fori_loop(0, qi, lambda ki, c: block(ki, c, False), init)
    _, l0, _, l1, acc = block(qi, carry, True)
    o_ref[...] = acc / jnp.where(first, l0, l1)


def _fox_attn(q, k, v, cq, ck, *, tq):
    B, T, D = q.shape
    hp = D // LANES
    qs = pl.BlockSpec((None, tq, LANES), lambda b, p, i: (b, i, p))
    kv = pl.BlockSpec((None, T, LANES), lambda b, p, i: (b, 0, p))
    cqs = pl.BlockSpec((None, None, tq, 2), lambda b, p, i: (b, p, i, 0))
    cks = pl.BlockSpec((None, None, T // tq, 2, tq), lambda b, p, i: (b, p, 0, 0, 0))
    return pl.pallas_call(
        functools.partial(_fox_attn_kernel, tq=tq), grid=(B, hp, T // tq),
        in_specs=[qs, kv, kv, cqs, cks], out_specs=qs, out_shape=jax.ShapeDtypeStruct((B, T, D), F32),
        compiler_params=_params(("arbitrary",) * 3), name="fox_attn")(q, k, v, cq, ck)


def _fox_paged_kernel(*refs, npg):
    it = iter(refs)
    _pt_ref = next(it)
    k_refs = [next(it) for _ in range(npg)]
    v_refs = [next(it) for _ in range(npg)]
    lf_refs = [next(it) for _ in range(npg)]
    q_ref, kn_ref, vn_ref, lfn_ref, o_ref, qbd_scr, m_scr, l_scr, carry_scr, acc_scr = (next(it) for _ in range(10))
    g = pl.program_id(1)
    D = q_ref.shape[1]
    H, N, P = FX_HEADS, FX_HEAD, PAGE_SIZE
    expand = jnp.where(_iota((H, D), 0) == _iota((H, D), 1) // N, 1.0, 0.0).astype(BF16)

    @pl.when(g == 0)
    def _():
        eye = jnp.where(_iota((LANES, LANES), 0) == _iota((LANES, LANES), 1), 1.0, 0.0)
        ones = jnp.ones((LANES, H), BF16)
        for j in range(D // LANES):
            qcol = _sel_dot(eye * q_ref[:, j * LANES:(j + 1) * LANES], ones)
            own = (_iota((LANES, H), 0) + j * LANES) // N == _iota((LANES, H), 1)
            qbd_scr[j * LANES:(j + 1) * LANES, :] = jnp.where(own, qcol, 0.0).astype(BF16)
        m_scr[...] = jnp.full_like(m_scr, NEG_BIG)
        l_scr[...] = jnp.zeros_like(l_scr)
        carry_scr[...] = jnp.zeros_like(carry_scr)
        acc_scr[...] = jnp.zeros_like(acc_scr)

    later = jnp.where(_iota((P, P), 1) > _iota((P, P), 0), 1.0, 0.0).astype(BF16)
    cq = lfn_ref[...]
    carry = carry_scr[...]
    scores = [None] * npg
    for j in reversed(range(npg)):
        lf = lf_refs[j][...]
        s = jnp.dot(k_refs[j][...].astype(BF16), qbd_scr[...], preferred_element_type=F32)
        scores[j] = s + _sel_dot_left(later, lf) + (carry + cq)
        carry = carry + jnp.sum(lf, axis=0, keepdims=True)
    carry_scr[...] = carry
    m_old = m_scr[...]
    m_new = m_old
    for j in range(npg):
        m_new = jnp.maximum(m_new, jnp.max(scores[j], axis=0, keepdims=True))
    alpha = jnp.exp(m_old - m_new)
    l_new = l_scr[...] * alpha
    part = jnp.zeros((8, D), F32)
    for j in range(npg):
        p = jnp.exp(scores[j] - m_new)
        l_new = l_new + jnp.sum(p, axis=0, keepdims=True)
        pe = jnp.dot(p.astype(BF16), expand, preferred_element_type=F32) * v_refs[j][...]
        part = part + jnp.sum(pe.reshape(P // 8, 8, D), axis=0)
    alpha_d = _sel_dot(jnp.broadcast_to(alpha, (8, H)), expand)
    acc_scr[...] = acc_scr[...] * alpha_d + part
    m_scr[...] = m_new
    l_scr[...] = l_new

    @pl.when(g == pl.num_programs(1) - 1)
    def _():
        seg = jnp.where(_iota((D, H), 0) // N == _iota((D, H), 1), 1.0, 0.0).astype(BF16)
        s_new = _sel_dot(jnp.broadcast_to(q_ref[...] * kn_ref[...], (8, D)), seg)[0:1, :]
        m_fin = jnp.maximum(m_new, s_new)
        a_fin = jnp.exp(m_new - m_fin)
        p_new = jnp.exp(s_new - m_fin)
        l_fin = l_new * a_fin + p_new
        stack = jnp.concatenate([a_fin / l_fin, p_new / l_fin, jnp.zeros((6, H), F32)], axis=0)
        wts = _sel_dot(stack, expand)
        acc = jnp.sum(acc_scr[...], axis=0, keepdims=True)
        o_ref[...] = acc * wts[0:1, :] + vn_ref[...] * wts[1:2, :]


def _fox_paged(q, k_new, v_new, lf_new, cache_k, cache_v, cache_lf, page_table, *, npg):
    B, _, D = q.shape
    n_pages = page_table.shape[1]
    groups = n_pages // npg

    def page(j):
        return lambda b, g, pt: (pt[b, (groups - 1 - g) * npg + j], 0, 0)

    kspecs = [pl.BlockSpec((None, PAGE_SIZE, D), page(j)) for j in range(npg)]
    lspecs = [pl.BlockSpec((None, PAGE_SIZE, FX_HEADS), page(j)) for j in range(npg)]
    tok = pl.BlockSpec((None, 1, D), lambda b, g, pt: (b, 0, 0))
    tokh = pl.BlockSpec((None, 1, FX_HEADS), lambda b, g, pt: (b, 0, 0))
    grid_spec = pltpu.PrefetchScalarGridSpec(
        num_scalar_prefetch=1, grid=(B, groups),
        in_specs=kspecs + kspecs + lspecs + [tok, tok, tok, tokh], out_specs=tok,
        scratch_shapes=[pltpu.VMEM((D, FX_HEADS), BF16), pltpu.VMEM((1, FX_HEADS), F32),
                        pltpu.VMEM((1, FX_HEADS), F32), pltpu.VMEM((1, FX_HEADS), F32),
                        pltpu.VMEM((8, D), F32)])
    return pl.pallas_call(
        functools.partial(_fox_paged_kernel, npg=npg), grid_spec=grid_spec,
        out_shape=jax.ShapeDtypeStruct((B, 1, D), F32),
        compiler_params=_params(("arbitrary", "arbitrary")), name="fox_paged")(
            page_table, *([cache_k] * npg), *([cache_v] * npg), *([cache_lf] * npg), q, k_new, v_new, lf_new)


def _row_tile(m, want):
    return want if m % want == 0 else m


def kernel(x_prompt, x_sample, state_rwkv_wkv, state_rwkv_shift, state_hgrn, cache_fox_k, cache_fox_v, cache_fox_logf, page_table, norm_pre_mix, norm_post_mix, norm_pre_ffn, norm_post_ffn, rw_mu, rw_wr, rw_wk, rw_wv, rw_wo, rw_w0, rw_w1, rw_w2, rw_a0, rw_a1, rw_a2, rw_v0, rw_v1, rw_v2, rw_g1, rw_g2, rw_kk, rw_ka, rw_rk, rw_lnw, rw_lnb, hg_w_in, hg_lb, hg_norm, hg_wo, fx_w_in, fx_bf, fx_qn, fx_kn, fx_wo, ffn_wg, ffn_wu, ffn_wd):
    D = D_MODEL
    bf = lambda t: t.astype(BF16)
    vec = lambda t: t.reshape(1, -1).astype(F32)

    n_cache_pages = cache_fox_k.shape[1]
    cache_k = cache_fox_k.reshape(-1, PAGE_SIZE, D)
    cache_v = cache_fox_v.reshape(-1, PAGE_SIZE, D)
    cache_lf = cache_fox_logf.reshape(-1, PAGE_SIZE, FX_HEADS)

    def trunk(x3, wkv0, shift0, hg0, paged):
        B, T, _ = x3.shape
        M = B * T
        x = x3.reshape(M, D)
        tm = _row_tile(M, 256)
        v_first = None
        wkv_n, shift_n, hg_n, k_n, v_n, lf_n = [], [], [], [], [], []
        for i in range(DEPTH):
            j = i // N_MIXERS
            gpre, gpost = vec(norm_pre_mix[i]), vec(norm_post_mix[i])
            if i % N_MIXERS == 0:
                vecs = [rw_w0[j], rw_a0[j], rw_kk[j], rw_ka[j]]
                vecs += [rw_v0[j - 1]] if j > 0 else []
                vecs = jnp.stack(vecs + [jnp.zeros((D,), F32)] * (8 - len(vecs)))
                mats = [bf(t[j]) for t in (rw_wr, rw_wk, rw_wv, rw_w1, rw_w2, rw_a1, rw_a2, rw_g1, rw_g2)]
                vres = None if j == 0 else (bf(rw_v1[j - 1]), bf(rw_v2[j - 1]), v_first)
                r, w, k, v, kk, kb, g, shift = _rwkv_proj(x, shift0[j], gpre, rw_mu[j], vecs, mats, vres,
                                                          seq_len=T, tm=tm)
                if j == 0:
                    v_first = v
                seq = lambda t: t.reshape(B, T, D)
                o, s_new = _wkv(seq(r), seq(w), seq(k), seq(v), seq(kk), seq(kb), wkv0[j], tblk=min(T, 256))
                ovec = jnp.stack([rw_rk[j].reshape(-1), rw_lnw[j], rw_lnb[j]] + [jnp.zeros((D,), F32)] * 5)
                x = _mixer_out(_rwkv_out_kernel, [o.reshape(M, D), r, k, v, g, x],
                               [ovec, bf(rw_wo[j]), gpost], tm=tm, name="rwkv_out")
                wkv_n.append(s_new)
                shift_n.append(shift)
            elif i % N_MIXERS == 1:
                q, k, v, lf, g = _hgrn_proj(x, gpre, hg_lb, bf(hg_w_in[j]), layer=i, tm=tm)
                seq = lambda t: t.reshape(B, T, D)
                s0t = jnp.swapaxes(hg0[j], -1, -2)
                if T == 1:
                    o, st = _gla_step(seq(q), seq(k), seq(v), seq(lf), s0t)
                else:
                    o, st = _gla(seq(q), seq(k), seq(v), seq(lf), s0t, tblk=min(T, 256))
                x = _mixer_out(_hgrn_out_kernel, [o.reshape(M, D), g, x],
                               [vec(hg_norm[j]), bf(hg_wo[j]), gpost], tm=tm, name="hgrn_out")
                hg_n.append(jnp.swapaxes(st, -1, -2))
            else:
                w_in = fx_w_in[j]
                w_f = jnp.pad(w_in[:, 4 * D:], ((0, 0), (0, LANES - FX_HEADS)))
                b_f = jnp.pad(fx_bf[j], (0, LANES - FX_HEADS)).reshape(1, LANES)
                tile = lambda t: jnp.tile(t, FX_HEADS).reshape(1, D)
                q, k, v, g, lf, c = _fox_proj(x, gpre, bf(w_in[:, :4 * D]), bf(w_f), b_f,
                                              tile(fx_qn[j]), tile(fx_kn[j]), seq_len=T, tm=tm)
                seq = lambda t: t.reshape(B, T, D)
                if paged:
                    pt = page_table + j * n_cache_pages
                    att = _fox_paged(seq(q), seq(k), seq(v), lf.reshape(B, 1, FX_HEADS),
                                     cache_k, cache_v, cache_lf, pt, npg=8)
                else:
                    tq = 256
                    c4 = c.reshape(B, T, FX_HEADS // 2, 2)
                    c5 = c.reshape(B, T // tq, tq, FX_HEADS // 2, 2)
                    att = _fox_attn(seq(q), seq(k), seq(v), jnp.transpose(c4, (0, 2, 1, 3)),
                                    jnp.transpose(c5, (0, 3, 1, 4, 2)), tq=tq)
                x = _mixer_out(_fox_out_kernel, [att.reshape(M, D), g, x], [bf(fx_wo[j]), gpost],
                               tm=tm, name="fox_out")
                k_n.append(k.reshape(B, T, FX_HEADS, FX_HEAD))
                v_n.append(v.reshape(B, T, FX_HEADS, FX_HEAD))
                lf_n.append(lf.reshape(B, T, FX_HEADS))
            x = _ffn(x, vec(norm_pre_ffn[i]), vec(norm_post_ffn[i]), bf(ffn_wg[i]), bf(ffn_wu[i]),
                     bf(ffn_wd[i]), tm=tm)
        return (x.reshape(B, T, D), jnp.stack(wkv_n), jnp.stack(shift_n), jnp.stack(hg_n),
                jnp.stack(k_n), jnp.stack(v_n), jnp.stack(lf_n))

    bp = x_prompt.shape[0]
    n_a, n_b = state_rwkv_wkv.shape[0], state_hgrn.shape[0]
    y_p, wkv_p, shift_p, hg_p, k_p, v_p, lf_p = trunk(
        x_prompt, jnp.zeros((n_a, bp) + state_rwkv_wkv.shape[2:], F32), jnp.zeros((n_a, bp, D), F32),
        jnp.zeros((n_b, bp) + state_hgrn.shape[2:], F32), False)
    y_s, wkv_s, shift_s, hg_s, k_s, v_s, lf_s = trunk(
        x_sample, state_rwkv_wkv, state_rwkv_shift, state_hgrn, True)
    return (y_p, y_s, wkv_p, wkv_s, shift_p, shift_s, hg_p, hg_s, k_p, k_s, v_p, v_s, lf_p, lf_s)
```

```python
import functools

import jax
import jax.numpy as jnp
from jax import lax
from jax.experimental import pallas as pl
from jax.experimental.pallas import tpu as pltpu

F32 = jnp.float32
BF16 = jnp.bfloat16

LANES = 128
VMEM_LIMIT = 56 * 1024 * 1024

D_MODEL = 1024
DEPTH = 4
N_MIXERS = 3
RW_HEAD = 64
RW_HEADS = D_MODEL // RW_HEAD
RW_DECAY_SCALE = 0.6065306597126334
RW_GN_EPS = 64e-5
HG_HEADS = 8
HG_HEAD = D_MODEL // HG_HEADS
HG_CHUNK = 16
FX_HEAD = 64
FX_HEADS = D_MODEL // FX_HEAD
PAGE_SIZE = 128
D_FF = 2816
NORM_EPS = 1e-6
NEG_BIG = -1e30
LOG2E = 1.4426950408889634


def _params(sem):
    return pltpu.CompilerParams(dimension_semantics=sem, vmem_limit_bytes=VMEM_LIMIT)


def _const_spec(shape):
    n = len(shape)
    return pl.BlockSpec(shape, lambda *_: (0,) * n, pipeline_mode=pl.Buffered(1))


def _rms(x, g, eps=NORM_EPS):
    return x * lax.rsqrt(jnp.mean(x * x, axis=-1, keepdims=True) + eps) * g


def _sigmoid(x):
    return 1.0 / (1.0 + jnp.exp(-x))


def _bdot(a, b):
    return jnp.dot(a.astype(BF16), b.astype(BF16), preferred_element_type=F32)


def _dot_nt(a, b):
    return lax.dot_general(a.astype(BF16), b.astype(BF16), (((1,), (1,)), ((), ())),
                           preferred_element_type=F32)


def _dot_tn(a, b):
    return lax.dot_general(a.astype(BF16), b.astype(BF16), (((0,), (0,)), ((), ())),
                           preferred_element_type=F32)


def _split(x):
    hi = x.astype(BF16)
    lo = (x - hi.astype(F32)).astype(BF16)
    return hi, lo


def _sel_dot(x, sel):
    hi, lo = _split(x)
    return (jnp.dot(hi, sel, preferred_element_type=F32)
            + jnp.dot(lo, sel, preferred_element_type=F32))


def _sel_dot_left(sel, x):
    hi, lo = _split(x)
    return (jnp.dot(sel, hi, preferred_element_type=F32)
            + jnp.dot(sel, lo, preferred_element_type=F32))


def _iota(shape, dim):
    return lax.broadcasted_iota(jnp.int32, shape, dim)


def _block_ones(n, seg):
    same = (_iota((n, n), 0) // seg) == (_iota((n, n), 1) // seg)
    return jnp.where(same, 1.0, 0.0).astype(BF16)


def _seg_sum(x, seg):
    sel = _block_ones(LANES, seg)
    parts = [_sel_dot(x[:, j * LANES:(j + 1) * LANES], sel) for j in range(x.shape[1] // LANES)]
    return jnp.concatenate(parts, axis=1)


def _rwkv_proj_kernel(*refs, seq_tiles, has_vres, single_step):
    it = iter(refs)
    x_ref, shift_ref, gpre_ref, mu_ref, vec_ref = (next(it) for _ in range(5))
    wr_ref, wk_ref, wv_ref, w1_ref, w2_ref, a1_ref, a2_ref, g1_ref, g2_ref = (next(it) for _ in range(9))
    if has_vres:
        v1_ref, v2_ref, vfirst_ref = (next(it) for _ in range(3))
    r_out, w_out, k_out, v_out, kk_out, b_out, g_out, shift_out = (next(it) for _ in range(8))
    last_scr = next(it)

    h = _rms(x_ref[...], gpre_ref[...])
    tm = h.shape[0]
    if single_step:
        prev = shift_ref[...]
        shift_out[...] = h
    else:
        i = pl.program_id(0)
        b = i // seq_tiles
        ti = i % seq_tiles
        first = jnp.where(ti == 0, shift_ref[pl.ds(b, 1), :], last_scr[...])
        prev = jnp.where(_iota((tm, 1), 0) == 0, first, pltpu.roll(h, 1, 0))
        last_scr[...] = h[tm - 1:tm, :]

        @pl.when(ti == seq_tiles - 1)
        def _():
            shift_out[pl.ds(b, 1), :] = h[tm - 1:tm, :]

    d = prev - h
    mix = lambda n: h + d * mu_ref[n:n + 1, :]
    w0, a0, k_k, k_a = (vec_ref[n:n + 1, :] for n in range(4))

    xv = mix(3)
    r = _bdot(mix(0), wr_ref[...])
    k = _bdot(mix(2), wk_ref[...])
    v = _bdot(xv, wv_ref[...])
    wl = w0 + _bdot(jnp.tanh(_bdot(mix(1), w1_ref[...])), w2_ref[...])
    a = _sigmoid(a0 + _bdot(_bdot(mix(4), a1_ref[...]), a2_ref[...]))
    if has_vres:
        gate = _sigmoid(vec_ref[4:5, :] + _bdot(_bdot(xv, v1_ref[...]), v2_ref[...]))
        v = v + (vfirst_ref[...] - v) * gate
    g = _bdot(_sigmoid(_bdot(mix(5), g1_ref[...])), g2_ref[...])

    kk = k * k_k
    kk = kk / jnp.maximum(jnp.sqrt(_seg_sum(kk * kk, RW_HEAD)), 1e-12)
    r_out[...] = r
    w_out[...] = -RW_DECAY_SCALE * _sigmoid(wl)
    k_out[...] = k * (1.0 + (a - 1.0) * k_a)
    v_out[...] = v
    kk_out[...] = kk
    b_out[...] = kk * a
    g_out[...] = g


def _rwkv_proj(x, shift0, gpre, mu, vecs, mats, vres, *, seq_len, tm):
    M, D = x.shape
    B = M // seq_len
    single_step = seq_len == 1
    seq_tiles = 1 if single_step else seq_len // tm
    row = pl.BlockSpec((tm, D), lambda i: (i, 0))
    shift_spec = row if single_step else _const_spec((B, D))
    shift_out_spec = row if single_step else pl.BlockSpec((B, D), lambda i: (0, 0))
    ins = [x, shift0, gpre, mu, vecs] + list(mats)
    specs = [row, shift_spec, _const_spec(gpre.shape), _const_spec(mu.shape), _const_spec(vecs.shape)]
    specs += [_const_spec(m.shape) for m in mats]
    if vres is not None:
        v1, v2, vfirst = vres
        ins += [v1, v2, vfirst]
        specs += [_const_spec(v1.shape), _const_spec(v2.shape), row]
    outs = [jax.ShapeDtypeStruct((M, D), F32)] * 7 + [jax.ShapeDtypeStruct((B, D), F32)]
    return pl.pallas_call(
        functools.partial(_rwkv_proj_kernel, seq_tiles=seq_tiles, has_vres=vres is not None,
                          single_step=single_step),
        grid=(M // tm,), in_specs=specs, out_specs=[row] * 7 + [shift_out_spec], out_shape=outs,
        scratch_shapes=[pltpu.VMEM((1, D), F32)],
        compiler_params=_params(("arbitrary",)), name="rwkv_proj")(*ins)


def _wkv_step_kernel(r_ref, lw_ref, k_ref, v_ref, kk_ref, b_ref, s0_ref, o_ref, sout_ref, *, pairs):
    n = RW_HEAD
    ones2 = _block_ones(LANES, n)
    diag2 = jnp.where(_iota((n, LANES), 0) == (_iota((n, LANES), 1) % n), 1.0, 0.0)
    for p in range(pairs):
        sl = slice(p * LANES, (p + 1) * LANES)
        s = jnp.concatenate([s0_ref[2 * p], s0_ref[2 * p + 1]], axis=1)
        lhs = jnp.concatenate([s * kk_ref[:, sl], diag2 * v_ref[:, sl]], axis=0)
        res = jnp.dot(lhs.astype(BF16), ones2, preferred_element_type=F32)
        s = s * jnp.exp(lw_ref[:, sl]) - res[:n] * b_ref[:, sl] + res[n:] * k_ref[:, sl]
        ob = jnp.dot((s * r_ref[:, sl]).astype(BF16), ones2, preferred_element_type=F32)
        o_ref[:, sl] = jnp.sum(ob * diag2, axis=0, keepdims=True)
        sout_ref[2 * p] = s[:, :n]
        sout_ref[2 * p + 1] = s[:, n:]


def _wkv_step(r, lw, k, v, kk, b, s0):
    B, _, D = r.shape
    H = s0.shape[1]
    seq = pl.BlockSpec((None, 1, D), lambda bi: (bi, 0, 0))
    st = pl.BlockSpec((None, H, RW_HEAD, RW_HEAD), lambda bi: (bi, 0, 0, 0))
    return pl.pallas_call(
        functools.partial(_wkv_step_kernel, pairs=H // 2), grid=(B,),
        in_specs=[seq] * 6 + [st], out_specs=[seq, st],
        out_shape=[jax.ShapeDtypeStruct((B, 1, D), F32), jax.ShapeDtypeStruct(s0.shape, F32)],
        compiler_params=_params(("arbitrary",)), name="wkv_step")(r, lw, k, v, kk, b, s0)


WKV_CHUNK = 64
WKV_GROUP = 256 // RW_HEAD


def _wkv_chunk_kernel(r_ref, lw_ref, k_ref, v_ref, kk_ref, b_ref, s0_ref, o_ref, sout_ref, s_scr, *, tblk, groups):
    C, W = WKV_CHUNK, 256
    ti = pl.program_id(1)

    @pl.when(ti == 0)
    def _():
        s_scr[...] = s0_ref[...]

    blk = jnp.where((_iota((W, W), 0) // C) == (_iota((W, W), 1) // C), 1.0, 0.0)
    blkb = blk.astype(BF16)
    tcol, trow = _iota((C, W), 1) % C, _iota((C, W), 0)
    strict, incl = tcol < trow, tcol <= trow
    eye = jnp.where(tcol == trow, 1.0, 0.0)
    tri = jnp.where(_iota((C, C), 1) <= _iota((C, C), 0), 1.0, 0.0).astype(BF16)

    def bd(x):
        return jnp.concatenate([x.astype(BF16)] * (W // C), axis=0) * blkb

    def sbs(y):
        ym = y * blk
        return (ym[0:C] + ym[C:2 * C]) + (ym[2 * C:3 * C] + ym[3 * C:4 * C])

    mm = lambda a, b: jnp.dot(a.astype(BF16), b, preferred_element_type=F32)

    def chunk(c, carry):
        t0 = pl.multiple_of(c * C, C)
        G = range(groups)
        each = lambda f, *cols: [f(*args) for args in zip(*cols)]
        load = lambda ref: [ref[pl.ds(t0, C), g * W:(g + 1) * W] for g in G]
        r, lw, k, v, kk, b = (load(ref) for ref in (r_ref, lw_ref, k_ref, v_ref, kk_ref, b_ref))
        cum = each(lambda x: _sel_dot_left(tri, x), lw)
        tot = each(lambda x: x[C - 1:C, :], cum)
        e_neg = each(lambda x: jnp.exp(-x), cum)
        rt = each(lambda x, y: x * jnp.exp(y), r, cum)
        at = each(lambda x, y, z: x * jnp.exp(y - z), kk, cum, lw)
        lhs2 = each(lambda x, y: jnp.concatenate([x, y], axis=0).astype(BF16), at, rt)
        gb = each(lambda l, x, e: _dot_nt(l, bd(x * e)), lhs2, b, e_neg)
        gk = each(lambda l, x, e: _dot_nt(l, bd(x * e)), lhs2, k, e_neg)
        aab = each(lambda x: jnp.where(strict, x[:C], 0.0), gb)
        arb = each(lambda x: jnp.where(incl, x[C:], 0.0).astype(BF16), gb)
        a2k = each(lambda x: jnp.concatenate([jnp.where(strict, x[:C], 0.0), jnp.where(incl, x[C:], 0.0)], axis=0), gk)
        av = each(lambda x, y: mm(x, bd(y)), a2k, v)
        x = each(lambda n: eye - n, aab)
        pw = aab
        pbd = each(bd, pw)
        for _ in range(5):
            pw = each(mm, pw, pbd)
            pbd = each(bd, pw)
            x = each(lambda y, z: y + mm(y, z), x, pbd)
        xb = each(lambda y: y.astype(BF16), x)
        p = each(lambda y, z: mm(y, bd(z)), xb, at)
        q = each(lambda y, z: mm(y, bd(z[:C])), xb, av)
        rp = each(lambda y, a, z: y - mm(a, bd(z)), rt, arb, p)
        op = each(lambda y, a, z: y[C:] - mm(a, bd(z)), av, arb, q)
        e_end = each(lambda t, y: jnp.exp(t - y), tot, cum)
        kp = each(lambda y, e: y * e, k, e_end)
        bp = each(lambda y, e: y * e, b, e_end)
        m_s = each(lambda t, y, z: eye * jnp.exp(t) - sbs(_dot_tn(y, z)), tot, bp, p)
        n_s = each(lambda y, z, vv, qq: sbs(_dot_tn(jnp.concatenate([y, z], axis=0),
                                                    jnp.concatenate([vv, -qq], axis=0))), kp, bp, v, q)
        sbd = [bd(s_scr[g]) for g in G]
        for g in G:
            o_ref[pl.ds(t0, C), g * W:(g + 1) * W] = mm(rp[g], sbd[g]) + op[g]
        for g in G:
            s_scr[g] = mm(m_s[g], sbd[g]) + n_s[g]
        return carry

    lax.fori_loop(0, tblk // C, chunk, 0)

    @pl.when(ti == pl.num_programs(1) - 1)
    def _():
        sout_ref[...] = s_scr[...]


def _wkv_chunked(r, lw, k, v, kk, b, s0, *, tblk):
    B, T, D = r.shape
    H, N = s0.shape[1], s0.shape[2]
    G = WKV_GROUP
    pack = lambda s: jnp.transpose(s.reshape(B, H // G, G, N, N), (0, 1, 4, 2, 3)).reshape(B, H // G, N, G * N)
    unpack = lambda s: jnp.transpose(s.reshape(B, H // G, N, G, N), (0, 1, 3, 4, 2)).reshape(B, H, N, N)
    seq = pl.BlockSpec((None, tblk, D), lambda bi, ti: (bi, ti, 0))
    st = pl.BlockSpec((None, H // G, N, G * N), lambda bi, ti: (bi, 0, 0, 0))
    o, s_new = pl.pallas_call(
        functools.partial(_wkv_chunk_kernel, tblk=tblk, groups=H // G),
        grid=(B, T // tblk), in_specs=[seq] * 6 + [st], out_specs=[seq, st],
        out_shape=[jax.ShapeDtypeStruct((B, T, D), F32), jax.ShapeDtypeStruct((B, H // G, N, G * N), F32)],
        scratch_shapes=[pltpu.VMEM((H // G, N, G * N), F32)],
        compiler_params=_params(("arbitrary", "arbitrary")), name="wkv_chunked")(r, lw, k, v, kk, b, pack(s0))
    return o, unpack(s_new)


def _finish(y, wo_ref, x_ref, gpost_ref, out_ref):
    z = jnp.dot(y.astype(BF16), wo_ref[...], preferred_element_type=F32)
    out_ref[...] = x_ref[...] + _rms(z, gpost_ref[...])


def _rwkv_out_kernel(o_ref, r_ref, k_ref, v_ref, g_ref, x_ref, vec_ref, wo_ref, gpost_ref, out_ref):
    rk, lnw, lnb = (vec_ref[n:n + 1, :] for n in range(3))
    o = o_ref[...]
    dev = o - _seg_sum(o, RW_HEAD) * (1.0 / RW_HEAD)
    var = _seg_sum(dev * dev, RW_HEAD) * (1.0 / RW_HEAD)
    on = dev * lax.rsqrt(var + RW_GN_EPS) * lnw + lnb
    bonus = _seg_sum(r_ref[...] * k_ref[...] * rk, RW_HEAD) * v_ref[...]
    _finish((on + bonus) * g_ref[...], wo_ref, x_ref, gpost_ref, out_ref)


def _hgrn_out_kernel(o_ref, g_ref, x_ref, ng_ref, wo_ref, gpost_ref, out_ref):
    o = o_ref[...]
    parts = []
    for j in range(o.shape[1] // HG_HEAD):
        oj = o[:, j * HG_HEAD:(j + 1) * HG_HEAD]
        parts.append(oj * lax.rsqrt(jnp.mean(oj * oj, axis=-1, keepdims=True) + NORM_EPS))
    g = g_ref[...]
    y = jnp.concatenate(parts, axis=1) * ng_ref[...] * (g * _sigmoid(g))
    _finish(y, wo_ref, x_ref, gpost_ref, out_ref)


def _fox_out_kernel(o_ref, g_ref, x_ref, wo_ref, gpost_ref, out_ref):
    _finish(o_ref[...] * _sigmoid(g_ref[...]), wo_ref, x_ref, gpost_ref, out_ref)


def _mixer_out(body, rows, consts, *, tm, name):
    M, D = rows[0].shape
    row = pl.BlockSpec((tm, D), lambda i: (i, 0))
    return pl.pallas_call(
        body, grid=(M // tm,), in_specs=[row] * len(rows) + [_const_spec(c.shape) for c in consts],
        out_specs=row, out_shape=jax.ShapeDtypeStruct((M, D), F32),
        compiler_params=_params(("arbitrary",)), name=name)(*rows, *consts)


def _ffn_kernel(x_ref, gpre_ref, gpost_ref, wg_ref, wu_ref, wd_ref, out_ref):
    x = x_ref[...]
    h = _rms(x, gpre_ref[...]).astype(BF16)
    a = jnp.dot(h, wg_ref[...], preferred_element_type=F32)
    u = jnp.dot(h, wu_ref[...], preferred_element_type=F32)
    act = (a * _sigmoid(a) * u).astype(BF16)
    z = jnp.dot(act, wd_ref[...], preferred_element_type=F32)
    out_ref[...] = x + _rms(z, gpost_ref[...])


def _ffn(x, gpre, gpost, wg, wu, wd, *, tm):
    M, D = x.shape
    row = pl.BlockSpec((tm, D), lambda i: (i, 0))
    consts = [gpre, gpost, wg, wu, wd]
    return pl.pallas_call(
        _ffn_kernel, grid=(M // tm,), in_specs=[row] + [_const_spec(c.shape) for c in consts],
        out_specs=row, out_shape=jax.ShapeDtypeStruct((M, D), F32),
        compiler_params=_params(("arbitrary",)), name="ffn")(x, *consts)


def _hgrn_proj_kernel(x_ref, gpre_ref, lb_ref, w_ref, q_out, k_out, v_out, lf_out, g_out, *, layer):
    D = x_ref.shape[1]
    h = _rms(x_ref[...], gpre_ref[...]).astype(BF16)
    proj = lambda n: jnp.dot(h, w_ref[:, n * D:(n + 1) * D], preferred_element_type=F32)
    lbw = lb_ref[...]
    e = jnp.exp(lbw - jnp.max(lbw, axis=0, keepdims=True))
    lb = jnp.sum(e[1:layer + 1], axis=0, keepdims=True) / jnp.sum(e, axis=0, keepdims=True)
    q = proj(0)
    f = lb + (1.0 - lb) * _sigmoid(proj(1))
    q_out[...] = q * _sigmoid(q)
    k_out[...] = 1.0 - f
    lf_out[...] = jnp.log(f)
    v_out[...] = proj(2)
    g_out[...] = proj(3)


def _hgrn_proj(x, gpre, hg_lb, w_in, *, layer, tm):
    M, D = x.shape
    row = pl.BlockSpec((tm, D), lambda i: (i, 0))
    return pl.pallas_call(
        functools.partial(_hgrn_proj_kernel, layer=layer), grid=(M // tm,),
        in_specs=[row, _const_spec(gpre.shape), _const_spec(hg_lb.shape), _const_spec(w_in.shape)],
        out_specs=[row] * 5, out_shape=[jax.ShapeDtypeStruct((M, D), F32)] * 5,
        compiler_params=_params(("arbitrary",)), name="hgrn_proj")(x, gpre, hg_lb, w_in)


def _gla_kernel(q_ref, k_ref, v_ref, lf_ref, s0_ref, o_ref, sout_ref, s_scr, *, tblk, heads):
    ti = pl.program_id(1)
    C = HG_CHUNK

    @pl.when(ti == 0)
    def _():
        s_scr[...] = s0_ref[...]

    lower = _iota((C, C), 0) >= _iota((C, C), 1)
    tri = jnp.where(lower, 1.0, 0.0).astype(BF16)

    def chunk(c, carry):
        t0 = pl.multiple_of(c * C, C)
        hs = range(heads)
        each = lambda f, *cols: [f(*args) for args in zip(*cols)]
        load = lambda ref: [ref[pl.ds(t0, C), hd * HG_HEAD:(hd + 1) * HG_HEAD] for hd in hs]
        q, k, v, lf = load(q_ref), load(k_ref), load(v_ref), load(lf_ref)
        b = each(lambda x: _sel_dot_left(tri, x), lf)
        qd = each(lambda x, y: (x * jnp.exp(y)).astype(BF16), q, b)
        att = each(lambda x, y, z: jnp.where(lower, _dot_nt(x, y * jnp.exp(-z)), 0.0), qd, k, b)
        st = [s_scr[hd] for hd in hs]
        o1 = each(_dot_nt, qd, st)
        o2 = each(_bdot, att, v)
        for hd in hs:
            o_ref[pl.ds(t0, C), hd * HG_HEAD:(hd + 1) * HG_HEAD] = o1[hd] + o2[hd]
        kv = each(lambda x, y, z: _dot_tn(x, y * jnp.exp(z[C - 1:C, :] - z)), v, k, b)
        for hd in hs:
            s_scr[hd] = st[hd] * jnp.exp(b[hd][C - 1:C, :]) + kv[hd]
        return carry

    lax.fori_loop(0, tblk // C, chunk, 0)

    @pl.when(ti == pl.num_programs(1) - 1)
    def _():
        sout_ref[...] = s_scr[...]


def _gla(q, k, v, lf, s0t, *, tblk):
    B, T, D = q.shape
    H = s0t.shape[1]
    seq = pl.BlockSpec((None, tblk, D), lambda bi, ti: (bi, ti, 0))
    st = pl.BlockSpec((None, H, HG_HEAD, HG_HEAD), lambda bi, ti: (bi, 0, 0, 0))
    return pl.pallas_call(
        functools.partial(_gla_kernel, tblk=tblk, heads=H),
        grid=(B, T // tblk), in_specs=[seq] * 4 + [st], out_specs=[seq, st],
        out_shape=[jax.ShapeDtypeStruct((B, T, D), F32), jax.ShapeDtypeStruct(s0t.shape, F32)],
        scratch_shapes=[pltpu.VMEM((H, HG_HEAD, HG_HEAD), F32)],
        compiler_params=_params(("arbitrary", "arbitrary")), name="gla_chunked")(q, k, v, lf, s0t)


def _gla_step_kernel(q_ref, k_ref, v_ref, lf_ref, s0_ref, o_ref, sout_ref, *, heads):
    n = HG_HEAD
    eye = jnp.where(_iota((n, n), 0) == _iota((n, n), 1), 1.0, 0.0)
    ones = jnp.ones((n, n), BF16)
    for hd in range(heads):
        sl = slice(hd * n, (hd + 1) * n)
        vcol = _sel_dot(eye * v_ref[:, sl], ones)
        st = s0_ref[hd] * jnp.exp(lf_ref[:, sl]) + vcol * k_ref[:, sl]
        sout_ref[hd] = st
        o_ref[:, sl] = _dot_nt(jnp.broadcast_to(q_ref[:, sl], (8, n)), st)[0:1, :]


def _gla_step(q, k, v, lf, s0t):
    B, _, D = q.shape
    H = s0t.shape[1]
    seq = pl.BlockSpec((None, 1, D), lambda bi: (bi, 0, 0))
    st = pl.BlockSpec((None, H, HG_HEAD, HG_HEAD), lambda bi: (bi, 0, 0, 0))
    return pl.pallas_call(
        functools.partial(_gla_step_kernel, heads=H), grid=(B,),
        in_specs=[seq] * 4 + [st], out_specs=[seq, st],
        out_shape=[jax.ShapeDtypeStruct((B, 1, D), F32), jax.ShapeDtypeStruct(s0t.shape, F32)],
        compiler_params=_params(("arbitrary",)), name="gla_step")(q, k, v, lf, s0t)


def _fox_proj_kernel(x_ref, gpre_ref, w_ref, wf_ref, bf_ref, qn_ref, kn_ref,
                     q_out, k_out, v_out, g_out, lf_out, c_out, carry_scr, *, seq_tiles, single_step):
    D = x_ref.shape[1]
    tm = x_ref.shape[0]
    h = _rms(x_ref[...], gpre_ref[...]).astype(BF16)
    proj = lambda n: jnp.dot(h, w_ref[:, n * D:(n + 1) * D], preferred_element_type=F32)
    head_norm = lambda t, gain: t * lax.rsqrt(_seg_sum(t * t, FX_HEAD) * (1.0 / FX_HEAD) + NORM_EPS) * gain
    q_out[...] = head_norm(proj(0), qn_ref[...]) * (FX_HEAD ** -0.5)
    k_out[...] = head_norm(proj(1), kn_ref[...])
    v_out[...] = proj(2)
    g_out[...] = proj(3)
    z = jnp.dot(h, wf_ref[...], preferred_element_type=F32) + bf_ref[...]
    lf = jnp.minimum(z, 0.0) - jnp.log1p(jnp.exp(-jnp.abs(z)))
    lf_out[...] = lf[:, :FX_HEADS]
    if single_step:
        c_out[...] = lf[:, :FX_HEADS]
    else:
        ti = pl.program_id(0) % seq_tiles

        @pl.when(ti == 0)
        def _():
            carry_scr[...] = jnp.zeros_like(carry_scr)

        tri = jnp.where(_iota((tm, tm), 0) >= _iota((tm, tm), 1), 1.0, 0.0).astype(BF16)
        c = _sel_dot_left(tri, lf) + carry_scr[...]
        carry_scr[...] = c[tm - 1:tm, :]
        c_out[...] = c[:, :FX_HEADS]


def _fox_proj(x, gpre, w_main, w_f, b_f, qn, kn, *, seq_len, tm):
    M, D = x.shape
    single_step = seq_len == 1
    row = pl.BlockSpec((tm, D), lambda i: (i, 0))
    hrow = pl.BlockSpec((tm, FX_HEADS), lambda i: (i, 0))
    consts = [gpre, w_main, w_f, b_f, qn, kn]
    return pl.pallas_call(
        functools.partial(_fox_proj_kernel, seq_tiles=1 if single_step else seq_len // tm,
                          single_step=single_step),
        grid=(M // tm,), in_specs=[row] + [_const_spec(c.shape) for c in consts],
        out_specs=[row] * 4 + [hrow] * 2,
        out_shape=[jax.ShapeDtypeStruct((M, D), F32)] * 4 + [jax.ShapeDtypeStruct((M, FX_HEADS), F32)] * 2,
        scratch_shapes=[pltpu.VMEM((1, LANES), F32)],
        compiler_params=_params(("arbitrary",)), name="fox_proj")(x, *consts)


def _fox_attn_kernel(q_ref, k_ref, v_ref, cq_ref, ck_ref, o_ref, *, tq, tk):
    qi = pl.program_id(2)
    first = _iota((1, LANES), 1) < FX_HEAD
    q = q_ref[...] * LOG2E
    q0 = jnp.where(first, q, 0.0).astype(BF16)
    q1 = jnp.where(first, 0.0, q).astype(BF16)
    cq = cq_ref[...] * LOG2E
    cq0, cq1 = cq[:, 0:1], cq[:, 1:2]
    full_blocks = (qi * tq) // tk

    def block(ki, carry, diagonal):
        m0, l0, m1, l1, acc = carry
        k0 = pl.multiple_of(ki * tk, tk)
        kb = k_ref[pl.ds(k0, tk), :].astype(BF16)
        vb = v_ref[pl.ds(k0, tk), :]
        ck = ck_ref[ki] * LOG2E
        s0 = _dot_nt(q0, kb) + cq0 - ck[0:1, :]
        s1 = _dot_nt(q1, kb) + cq1 - ck[1:2, :]
        if diagonal:
            keep = (_iota((tq, tk), 1) - _iota((tq, tk), 0)) <= (qi * tq - k0)
            s0 = jnp.where(keep, s0, NEG_BIG)
            s1 = jnp.where(keep, s1, NEG_BIG)
        n0 = jnp.maximum(m0, jnp.max(s0, axis=-1, keepdims=True))
        n1 = jnp.maximum(m1, jnp.max(s1, axis=-1, keepdims=True))
        p0 = jnp.exp2(s0 - n0)
        p1 = jnp.exp2(s1 - n1)
        a0 = jnp.exp2(m0 - n0)
        a1 = jnp.exp2(m1 - n1)
        l0 = l0 * a0 + jnp.sum(p0, axis=-1, keepdims=True)
        l1 = l1 * a1 + jnp.sum(p1, axis=-1, keepdims=True)
        pv = (_bdot(p0, jnp.where(first, vb, 0.0)) + _bdot(p1, jnp.where(first, 0.0, vb)))
        acc = acc * jnp.where(first, a0, a1) + pv
        return n0, l0, n1, l1, acc

    col = lambda val: jnp.full((tq, 1), val, F32)
    init = (col(NEG_BIG), col(0.0), col(NEG_BIG), col(0.0), jnp.zeros((tq, LANES), F32))
    carry = lax.fori_loop(0, full_blocks, lambda ki, c: block(ki, c, False), init)
    _, l0, _, l1, acc = block(full_blocks, carry, True)
    o_ref[...] = acc / jnp.where(first, l0, l1)


def _fox_attn(q, k, v, c, *, tq, tk):
    B, T, D = q.shape
    hp = D // LANES
    cq = jnp.transpose(c.reshape(B, T, hp, 2), (0, 2, 1, 3))
    ck = jnp.transpose(c.reshape(B, T // tk, tk, hp, 2), (0, 3, 1, 4, 2))
    qs = pl.BlockSpec((None, tq, LANES), lambda b, p, i: (b, i, p))
    kv = pl.BlockSpec((None, T, LANES), lambda b, p, i: (b, 0, p))
    cqs = pl.BlockSpec((None, None, tq, 2), lambda b, p, i: (b, p, i, 0))
    cks = pl.BlockSpec((None, None, T // tk, 2, tk), lambda b, p, i: (b, p, 0, 0, 0))
    return pl.pallas_call(
        functools.partial(_fox_attn_kernel, tq=tq, tk=tk), grid=(B, hp, T // tq),
        in_specs=[qs, kv, kv, cqs, cks], out_specs=qs, out_shape=jax.ShapeDtypeStruct((B, T, D), F32),
        compiler_params=_params(("arbitrary",) * 3), name="fox_attn")(q, k, v, cq, ck)


def _fox_paged_kernel(*refs, npg):
    it = iter(refs)
    _pt_ref = next(it)
    k_refs = [next(it) for _ in range(npg)]
    v_refs = [next(it) for _ in range(npg)]
    lf_refs = [next(it) for _ in range(npg)]
    q_ref, kn_ref, vn_ref, lfn_ref, o_ref, qbd_scr, m_scr, l_scr, carry_scr, acc_scr = (next(it) for _ in range(10))
    g = pl.program_id(1)
    D = q_ref.shape[1]
    H, N, P = FX_HEADS, FX_HEAD, PAGE_SIZE
    expand = jnp.where(_iota((H, D), 0) == _iota((H, D), 1) // N, 1.0, 0.0).astype(BF16)

    @pl.when(g == 0)
    def _():
        eye = jnp.where(_iota((LANES, LANES), 0) == _iota((LANES, LANES), 1), 1.0, 0.0)
        ones = jnp.ones((LANES, H), BF16)
        for j in range(D // LANES):
            qcol = _sel_dot(eye * q_ref[:, j * LANES:(j + 1) * LANES], ones)
            own = (_iota((LANES, H), 0) + j * LANES) // N == _iota((LANES, H), 1)
            qbd_scr[j * LANES:(j + 1) * LANES, :] = jnp.where(own, qcol, 0.0).astype(BF16)
        m_scr[...] = jnp.full_like(m_scr, NEG_BIG)
        l_scr[...] = jnp.zeros_like(l_scr)
        carry_scr[...] = jnp.zeros_like(carry_scr)
        acc_scr[...] = jnp.zeros_like(acc_scr)

    later = jnp.where(_iota((P, P), 1) > _iota((P, P), 0), 1.0, 0.0).astype(BF16)
    cq = lfn_ref[...]
    carry = carry_scr[...]
    scores = [None] * npg
    for j in reversed(range(npg)):
        lf = lf_refs[j][...]
        s = jnp.dot(k_refs[j][...].astype(BF16), qbd_scr[...], preferred_element_type=F32)
        scores[j] = s + _sel_dot_left(later, lf) + (carry + cq)
        carry = carry + jnp.sum(lf, axis=0, keepdims=True)
    carry_scr[...] = carry
    m_old = m_scr[...]
    m_new = m_old
    for j in range(npg):
        m_new = jnp.maximum(m_new, jnp.max(scores[j], axis=0, keepdims=True))
    alpha = jnp.exp(m_old - m_new)
    l_new = l_scr[...] * alpha
    part = jnp.zeros((8, D), F32)
    for j in range(npg):
        p = jnp.exp(scores[j] - m_new)
        l_new = l_new + jnp.sum(p, axis=0, keepdims=True)
        pe = jnp.dot(p.astype(BF16), expand, preferred_element_type=F32) * v_refs[j][...]
        part = part + jnp.sum(pe.reshape(P // 8, 8, D), axis=0)
    alpha_d = _sel_dot(jnp.broadcast_to(alpha, (8, H)), expand)
    acc_scr[...] = acc_scr[...] * alpha_d + part
    m_scr[...] = m_new
    l_scr[...] = l_new

    @pl.when(g == pl.num_programs(1) - 1)
    def _():
        seg = jnp.where(_iota((D, H), 0) // N == _iota((D, H), 1), 1.0, 0.0).astype(BF16)
        s_new = _sel_dot(jnp.broadcast_to(q_ref[...] * kn_ref[...], (8, D)), seg)[0:1, :]
        m_fin = jnp.maximum(m_new, s_new)
        a_fin = jnp.exp(m_new - m_fin)
        p_new = jnp.exp(s_new - m_fin)
        l_fin = l_new * a_fin + p_new
        stack = jnp.concatenate([a_fin / l_fin, p_new / l_fin, jnp.zeros((6, H), F32)], axis=0)
        wts = _sel_dot(stack, expand)
        acc = jnp.sum(acc_scr[...], axis=0, keepdims=True)
        o_ref[...] = acc * wts[0:1, :] + vn_ref[...] * wts[1:2, :]


def _fox_paged(q, k_new, v_new, lf_new, cache_k, cache_v, cache_lf, page_table, *, npg):
    B, _, D = q.shape
    n_pages = page_table.shape[1]
    groups = n_pages // npg

    def page(j):
        return lambda b, g, pt: (pt[b, (groups - 1 - g) * npg + j], 0, 0)

    kspecs = [pl.BlockSpec((None, PAGE_SIZE, D), page(j)) for j in range(npg)]
    lspecs = [pl.BlockSpec((None, PAGE_SIZE, FX_HEADS), page(j)) for j in range(npg)]
    tok = pl.BlockSpec((None, 1, D), lambda b, g, pt: (b, 0, 0))
    tokh = pl.BlockSpec((None, 1, FX_HEADS), lambda b, g, pt: (b, 0, 0))
    grid_spec = pltpu.PrefetchScalarGridSpec(
        num_scalar_prefetch=1, grid=(B, groups),
        in_specs=kspecs + kspecs + lspecs + [tok, tok, tok, tokh], out_specs=tok,
        scratch_shapes=[pltpu.VMEM((D, FX_HEADS), BF16), pltpu.VMEM((1, FX_HEADS), F32),
                        pltpu.VMEM((1, FX_HEADS), F32), pltpu.VMEM((1, FX_HEADS), F32),
                        pltpu.VMEM((8, D), F32)])
    return pl.pallas_call(
        functools.partial(_fox_paged_kernel, npg=npg), grid_spec=grid_spec,
        out_shape=jax.ShapeDtypeStruct((B, 1, D), F32),
        compiler_params=_params(("arbitrary", "arbitrary")), name="fox_paged")(
            page_table, *([cache_k] * npg), *([cache_v] * npg), *([cache_lf] * npg), q, k_new, v_new, lf_new)


def _row_tile(m, want):
    return want if m % want == 0 else m


def kernel(x_prompt, x_sample, state_rwkv_wkv, state_rwkv_shift, state_hgrn, cache_fox_k, cache_fox_v, cache_fox_logf, page_table, norm_pre_mix, norm_post_mix, norm_pre_ffn, norm_post_ffn, rw_mu, rw_wr, rw_wk, rw_wv, rw_wo, rw_w0, rw_w1, rw_w2, rw_a0, rw_a1, rw_a2, rw_v0, rw_v1, rw_v2, rw_g1, rw_g2, rw_kk, rw_ka, rw_rk, rw_lnw, rw_lnb, hg_w_in, hg_lb, hg_norm, hg_wo, fx_w_in, fx_bf, fx_qn, fx_kn, fx_wo, ffn_wg, ffn_wu, ffn_wd):
    D = D_MODEL
    bf = lambda t: t.astype(BF16)
    vec = lambda t: t.reshape(1, -1).astype(F32)

    n_cache_pages = cache_fox_k.shape[1]
    cache_k = cache_fox_k.reshape(-1, PAGE_SIZE, D)
    cache_v = cache_fox_v.reshape(-1, PAGE_SIZE, D)
    cache_lf = cache_fox_logf.reshape(-1, PAGE_SIZE, FX_HEADS)

    def trunk(x3, wkv0, shift0, hg0, paged):
        B, T, _ = x3.shape
        M = B * T
        x = x3.reshape(M, D)
        tm = _row_tile(M, 256)
        v_first = None
        wkv_n, shift_n, hg_n, k_n, v_n, lf_n = [], [], [], [], [], []
        for i in range(DEPTH):
            j = i // N_MIXERS
            gpre, gpost = vec(norm_pre_mix[i]), vec(norm_post_mix[i])
            if i % N_MIXERS == 0:
                vecs = [rw_w0[j], rw_a0[j], rw_kk[j], rw_ka[j]]
                vecs += [rw_v0[j - 1]] if j > 0 else []
                vecs = jnp.stack(vecs + [jnp.zeros((D,), F32)] * (8 - len(vecs)))
                mats = [bf(t[j]) for t in (rw_wr, rw_wk, rw_wv, rw_w1, rw_w2, rw_a1, rw_a2, rw_g1, rw_g2)]
                vres = None if j == 0 else (bf(rw_v1[j - 1]), bf(rw_v2[j - 1]), v_first)
                r, w, k, v, kk, kb, g, shift = _rwkv_proj(x, shift0[j], gpre, rw_mu[j], vecs, mats, vres,
                                                          seq_len=T, tm=tm)
                if j == 0:
                    v_first = v
                seq = lambda t: t.reshape(B, T, D)
                wkv = _wkv_step if T == 1 else functools.partial(_wkv_chunked, tblk=min(T, 256))
                o, s_new = wkv(seq(r), seq(w), seq(k), seq(v), seq(kk), seq(kb), wkv0[j])
                ovec = jnp.stack([rw_rk[j].reshape(-1), rw_lnw[j], rw_lnb[j]] + [jnp.zeros((D,), F32)] * 5)
                x = _mixer_out(_rwkv_out_kernel, [o.reshape(M, D), r, k, v, g, x],
                               [ovec, bf(rw_wo[j]), gpost], tm=tm, name="rwkv_out")
                wkv_n.append(s_new)
                shift_n.append(shift)
            elif i % N_MIXERS == 1:
                q, k, v, lf, g = _hgrn_proj(x, gpre, hg_lb, bf(hg_w_in[j]), layer=i, tm=tm)
                seq = lambda t: t.reshape(B, T, D)
                s0t = jnp.swapaxes(hg0[j], -1, -2)
                if T == 1:
                    o, st = _gla_step(seq(q), seq(k), seq(v), seq(lf), s0t)
                else:
                    o, st = _gla(seq(q), seq(k), seq(v), seq(lf), s0t, tblk=min(T, 256))
                x = _mixer_out(_hgrn_out_kernel, [o.reshape(M, D), g, x],
                               [vec(hg_norm[j]), bf(hg_wo[j]), gpost], tm=tm, name="hgrn_out")
                hg_n.append(jnp.swapaxes(st, -1, -2))
            else:
                w_in = fx_w_in[j]
                w_f = jnp.pad(w_in[:, 4 * D:], ((0, 0), (0, LANES - FX_HEADS)))
                b_f = jnp.pad(fx_bf[j], (0, LANES - FX_HEADS)).reshape(1, LANES)
                tile = lambda t: jnp.tile(t, FX_HEADS).reshape(1, D)
                q, k, v, g, lf, c = _fox_proj(x, gpre, bf(w_in[:, :4 * D]), bf(w_f), b_f,
                                              tile(fx_qn[j]), tile(fx_kn[j]), seq_len=T, tm=tm)
                seq = lambda t: t.reshape(B, T, D)
                if paged:
                    pt = page_table + j * n_cache_pages
                    att = _fox_paged(seq(q), seq(k), seq(v), lf.reshape(B, 1, FX_HEADS),
                                     cache_k, cache_v, cache_lf, pt, npg=8)
                else:
                    att = _fox_attn(seq(q), seq(k), seq(v), c.reshape(B, T, FX_HEADS), tq=256, tk=512)
                x = _mixer_out(_fox_out_kernel, [att.reshape(M, D), g, x], [bf(fx_wo[j]), gpost],
                               tm=tm, name="fox_out")
                k_n.append(k.reshape(B, T, FX_HEADS, FX_HEAD))
                v_n.append(v.reshape(B, T, FX_HEADS, FX_HEAD))
                lf_n.append(lf.reshape(B, T, FX_HEADS))
            x = _ffn(x, vec(norm_pre_ffn[i]), vec(norm_post_ffn[i]), bf(ffn_wg[i]), bf(ffn_wu[i]),
                     bf(ffn_wd[i]), tm=tm)
        return (x.reshape(B, T, D), jnp.stack(wkv_n), jnp.stack(shift_n), jnp.stack(hg_n),
                jnp.stack(k_n), jnp.stack(v_n), jnp.stack(lf_n))

    bp = x_prompt.shape[0]
    n_a, n_b = state_rwkv_wkv.shape[0], state_hgrn.shape[0]
    y_p, wkv_p, shift_p, hg_p, k_p, v_p, lf_p = trunk(
        x_prompt, jnp.zeros((n_a, bp) + state_rwkv_wkv.shape[2:], F32), jnp.zeros((n_a, bp, D), F32),
        jnp.zeros((n_b, bp) + state_hgrn.shape[2:], F32), False)
    y_s, wkv_s, shift_s, hg_s, k_s, v_s, lf_s = trunk(
        x_sample, state_rwkv_wkv, state_rwkv_shift, state_hgrn, True)
    return (y_p, y_s, wkv_p, wkv_s, shift_p, shift_s, hg_p, hg_s, k_p, k_s, v_p, v_s, lf_p, lf_s)
```

```python
import functools
import math

import jax
import jax.numpy as jnp
from jax import lax
from jax.experimental import pallas as pl
from jax.experimental.pallas import tpu as pltpu

F32 = jnp.float32
BF16 = jnp.bfloat16

LANES = 128
VMEM_LIMIT = 56 * 1024 * 1024

D_MODEL = 1024
DEPTH = 4
N_MIXERS = 3
RW_HEAD = 64
RW_HEADS = D_MODEL // RW_HEAD
RW_DECAY_SCALE = 0.6065306597126334
RW_GN_EPS = 64e-5
HG_HEADS = 8
HG_HEAD = D_MODEL // HG_HEADS
HG_CHUNK = 16
FX_HEAD = 64
FX_HEADS = D_MODEL // FX_HEAD
PAGE_SIZE = 128
D_FF = 2816
NORM_EPS = 1e-6
NEG_BIG = -1e30
LOG2E = 1.4426950408889634


def _params(sem):
    return pltpu.CompilerParams(dimension_semantics=sem, vmem_limit_bytes=VMEM_LIMIT)


def _const_spec(shape):
    n = len(shape)
    return pl.BlockSpec(shape, lambda *_: (0,) * n, pipeline_mode=pl.Buffered(1))


def _rms(x, g, eps=NORM_EPS):
    return x * lax.rsqrt(jnp.mean(x * x, axis=-1, keepdims=True) + eps) * g


def _sigmoid(x):
    return 1.0 / (1.0 + jnp.exp(-x))


def _bdot(a, b):
    return jnp.dot(a.astype(BF16), b.astype(BF16), preferred_element_type=F32)


def _dot_nt(a, b):
    return lax.dot_general(a.astype(BF16), b.astype(BF16), (((1,), (1,)), ((), ())),
                           preferred_element_type=F32)


def _dot_tn(a, b):
    return lax.dot_general(a.astype(BF16), b.astype(BF16), (((0,), (0,)), ((), ())),
                           preferred_element_type=F32)


def _split(x):
    hi = x.astype(BF16)
    lo = (x - hi.astype(F32)).astype(BF16)
    return hi, lo


def _sel_dot(x, sel):
    hi, lo = _split(x)
    return (jnp.dot(hi, sel, preferred_element_type=F32)
            + jnp.dot(lo, sel, preferred_element_type=F32))


def _sel_dot_left(sel, x):
    hi, lo = _split(x)
    return (jnp.dot(sel, hi, preferred_element_type=F32)
            + jnp.dot(sel, lo, preferred_element_type=F32))


def _iota(shape, dim):
    return lax.broadcasted_iota(jnp.int32, shape, dim)


def _block_ones(n, seg):
    same = (_iota((n, n), 0) // seg) == (_iota((n, n), 1) // seg)
    return jnp.where(same, 1.0, 0.0).astype(BF16)


def _seg_sum(x, seg):
    sel = _block_ones(LANES, seg)
    parts = [_sel_dot(x[:, j * LANES:(j + 1) * LANES], sel) for j in range(x.shape[1] // LANES)]
    return jnp.concatenate(parts, axis=1)


def _rwkv_proj_kernel(*refs, seq_tiles, has_vres, single_step):
    it = iter(refs)
    x_ref, shift_ref, gpre_ref, mu_ref, vec_ref = (next(it) for _ in range(5))
    wr_ref, wk_ref, wv_ref, w1_ref, w2_ref, a1_ref, a2_ref, g1_ref, g2_ref = (next(it) for _ in range(9))
    if has_vres:
        v1_ref, v2_ref, vfirst_ref = (next(it) for _ in range(3))
    r_out, w_out, k_out, v_out, kk_out, b_out, g_out, shift_out = (next(it) for _ in range(8))
    last_scr = next(it)

    h = _rms(x_ref[...], gpre_ref[...])
    tm = h.shape[0]
    if single_step:
        prev = shift_ref[...]
        shift_out[...] = h
    else:
        i = pl.program_id(0)
        b = i // seq_tiles
        ti = i % seq_tiles
        first = jnp.where(ti == 0, shift_ref[pl.ds(b, 1), :], last_scr[...])
        prev = jnp.where(_iota((tm, 1), 0) == 0, first, pltpu.roll(h, 1, 0))
        last_scr[...] = h[tm - 1:tm, :]

        @pl.when(ti == seq_tiles - 1)
        def _():
            shift_out[pl.ds(b, 1), :] = h[tm - 1:tm, :]

    d = prev - h
    mix = lambda n: h + d * mu_ref[n:n + 1, :]
    w0, a0, k_k, k_a = (vec_ref[n:n + 1, :] for n in range(4))

    xv = mix(3)
    r = _bdot(mix(0), wr_ref[...])
    k = _bdot(mix(2), wk_ref[...])
    v = _bdot(xv, wv_ref[...])
    wl = w0 + _bdot(jnp.tanh(_bdot(mix(1), w1_ref[...])), w2_ref[...])
    a = _sigmoid(a0 + _bdot(_bdot(mix(4), a1_ref[...]), a2_ref[...]))
    if has_vres:
        gate = _sigmoid(vec_ref[4:5, :] + _bdot(_bdot(xv, v1_ref[...]), v2_ref[...]))
        v = v + (vfirst_ref[...] - v) * gate
    g = _bdot(_sigmoid(_bdot(mix(5), g1_ref[...])), g2_ref[...])

    kk = k * k_k
    kk = kk / jnp.maximum(jnp.sqrt(_seg_sum(kk * kk, RW_HEAD)), 1e-12)
    r_out[...] = r
    w_out[...] = -RW_DECAY_SCALE * _sigmoid(wl)
    k_out[...] = k * (1.0 + (a - 1.0) * k_a)
    v_out[...] = v
    kk_out[...] = kk
    b_out[...] = kk * a
    g_out[...] = g


def _rwkv_proj(x, shift0, gpre, mu, vecs, mats, vres, *, seq_len, tm):
    M, D = x.shape
    B = M // seq_len
    single_step = seq_len == 1
    seq_tiles = 1 if single_step else seq_len // tm
    row = pl.BlockSpec((tm, D), lambda i: (i, 0))
    shift_spec = row if single_step else _const_spec((B, D))
    shift_out_spec = row if single_step else pl.BlockSpec((B, D), lambda i: (0, 0))
    ins = [x, shift0, gpre, mu, vecs] + list(mats)
    specs = [row, shift_spec, _const_spec(gpre.shape), _const_spec(mu.shape), _const_spec(vecs.shape)]
    specs += [_const_spec(m.shape) for m in mats]
    if vres is not None:
        v1, v2, vfirst = vres
        ins += [v1, v2, vfirst]
        specs += [_const_spec(v1.shape), _const_spec(v2.shape), row]
    outs = [jax.ShapeDtypeStruct((M, D), F32)] * 7 + [jax.ShapeDtypeStruct((B, D), F32)]
    return pl.pallas_call(
        functools.partial(_rwkv_proj_kernel, seq_tiles=seq_tiles, has_vres=vres is not None,
                          single_step=single_step),
        grid=(M // tm,), in_specs=specs, out_specs=[row] * 7 + [shift_out_spec], out_shape=outs,
        scratch_shapes=[pltpu.VMEM((1, D), F32)],
        compiler_params=_params(("arbitrary",)), name="rwkv_proj")(*ins)


def _wkv_step_kernel(r_ref, lw_ref, k_ref, v_ref, kk_ref, b_ref, s0_ref, o_ref, sout_ref, *, pairs):
    n = RW_HEAD
    ones2 = _block_ones(LANES, n)
    diag2 = jnp.where(_iota((n, LANES), 0) == (_iota((n, LANES), 1) % n), 1.0, 0.0)
    for p in range(pairs):
        sl = slice(p * LANES, (p + 1) * LANES)
        s = jnp.concatenate([s0_ref[2 * p], s0_ref[2 * p + 1]], axis=1)
        lhs = jnp.concatenate([s * kk_ref[:, sl], diag2 * v_ref[:, sl]], axis=0)
        res = jnp.dot(lhs.astype(BF16), ones2, preferred_element_type=F32)
        s = s * jnp.exp(lw_ref[:, sl]) - res[:n] * b_ref[:, sl] + res[n:] * k_ref[:, sl]
        ob = jnp.dot((s * r_ref[:, sl]).astype(BF16), ones2, preferred_element_type=F32)
        o_ref[:, sl] = jnp.sum(ob * diag2, axis=0, keepdims=True)
        sout_ref[2 * p] = s[:, :n]
        sout_ref[2 * p + 1] = s[:, n:]


def _wkv_step(r, lw, k, v, kk, b, s0):
    B, _, D = r.shape
    H = s0.shape[1]
    seq = pl.BlockSpec((None, 1, D), lambda bi: (bi, 0, 0))
    st = pl.BlockSpec((None, H, RW_HEAD, RW_HEAD), lambda bi: (bi, 0, 0, 0))
    return pl.pallas_call(
        functools.partial(_wkv_step_kernel, pairs=H // 2), grid=(B,),
        in_specs=[seq] * 6 + [st], out_specs=[seq, st],
        out_shape=[jax.ShapeDtypeStruct((B, 1, D), F32), jax.ShapeDtypeStruct(s0.shape, F32)],
        compiler_params=_params(("arbitrary",)), name="wkv_step")(r, lw, k, v, kk, b, s0)


WKV_CHUNK = 64
WKV_GROUP = 256 // RW_HEAD


def _wkv_chunk_kernel(r_ref, lw_ref, k_ref, v_ref, kk_ref, b_ref, s0_ref, o_ref, sout_ref, s_scr, *, tblk, groups):
    C, W = WKV_CHUNK, 256
    ti = pl.program_id(1)

    @pl.when(ti == 0)
    def _():
        s_scr[...] = s0_ref[...]

    blk = jnp.where((_iota((W, W), 0) // C) == (_iota((W, W), 1) // C), 1.0, 0.0)
    blkb = blk.astype(BF16)
    tcol, trow = _iota((C, W), 1) % C, _iota((C, W), 0)
    strict, incl = tcol < trow, tcol <= trow
    eye = jnp.where(tcol == trow, 1.0, 0.0)
    tri = jnp.where(_iota((C, C), 1) <= _iota((C, C), 0), 1.0, 0.0).astype(BF16)

    def bd(x):
        return jnp.concatenate([x.astype(BF16)] * (W // C), axis=0) * blkb

    def sbs(y):
        ym = y * blk
        return (ym[0:C] + ym[C:2 * C]) + (ym[2 * C:3 * C] + ym[3 * C:4 * C])

    mm = lambda a, b: jnp.dot(a.astype(BF16), b, preferred_element_type=F32)

    def chunk(c, carry):
        t0 = pl.multiple_of(c * C, C)
        G = range(groups)
        each = lambda f, *cols: [f(*args) for args in zip(*cols)]
        load = lambda ref: [ref[pl.ds(t0, C), g * W:(g + 1) * W] for g in G]
        r, lw, k, v, kk, b = (load(ref) for ref in (r_ref, lw_ref, k_ref, v_ref, kk_ref, b_ref))
        cum = each(lambda x: _sel_dot_left(tri, x), lw)
        tot = each(lambda x: x[C - 1:C, :], cum)
        e_neg = each(lambda x: jnp.exp(-x), cum)
        rt = each(lambda x, y: x * jnp.exp(y), r, cum)
        at = each(lambda x, y, z: x * jnp.exp(y - z), kk, cum, lw)
        lhs2 = each(lambda x, y: jnp.concatenate([x, y], axis=0).astype(BF16), at, rt)
        gb = each(lambda l, x, e: _dot_nt(l, bd(x * e)), lhs2, b, e_neg)
        gk = each(lambda l, x, e: _dot_nt(l, bd(x * e)), lhs2, k, e_neg)
        aab = each(lambda x: jnp.where(strict, x[:C], 0.0), gb)
        arb = each(lambda x: jnp.where(incl, x[C:], 0.0).astype(BF16), gb)
        a2k = each(lambda x: jnp.concatenate([jnp.where(strict, x[:C], 0.0), jnp.where(incl, x[C:], 0.0)], axis=0), gk)
        av = each(lambda x, y: mm(x, bd(y)), a2k, v)
        x = each(lambda n: eye - n, aab)
        pw = aab
        pbd = each(bd, pw)
        for _ in range(5):
            pw = each(mm, pw, pbd)
            pbd = each(bd, pw)
            x = each(lambda y, z: y + mm(y, z), x, pbd)
        xb = each(lambda y: y.astype(BF16), x)
        p = each(lambda y, z: mm(y, bd(z)), xb, at)
        q = each(lambda y, z: mm(y, bd(z[:C])), xb, av)
        rp = each(lambda y, a, z: y - mm(a, bd(z)), rt, arb, p)
        op = each(lambda y, a, z: y[C:] - mm(a, bd(z)), av, arb, q)
        e_end = each(lambda t, y: jnp.exp(t - y), tot, cum)
        kp = each(lambda y, e: y * e, k, e_end)
        bp = each(lambda y, e: y * e, b, e_end)
        m_s = each(lambda t, y, z: eye * jnp.exp(t) - sbs(_dot_tn(y, z)), tot, bp, p)
        n_s = each(lambda y, z, vv, qq: sbs(_dot_tn(jnp.concatenate([y, z], axis=0),
                                                    jnp.concatenate([vv, -qq], axis=0))), kp, bp, v, q)
        sbd = [bd(s_scr[g]) for g in G]
        for g in G:
            o_ref[pl.ds(t0, C), g * W:(g + 1) * W] = mm(rp[g], sbd[g]) + op[g]
        for g in G:
            s_scr[g] = mm(m_s[g], sbd[g]) + n_s[g]
        return carry

    lax.fori_loop(0, tblk // C, chunk, 0)

    @pl.when(ti == pl.num_programs(1) - 1)
    def _():
        sout_ref[...] = s_scr[...]


def _wkv_chunked(r, lw, k, v, kk, b, s0, *, tblk):
    B, T, D = r.shape
    H, N = s0.shape[1], s0.shape[2]
    G = WKV_GROUP
    pack = lambda s: jnp.transpose(s.reshape(B, H // G, G, N, N), (0, 1, 4, 2, 3)).reshape(B, H // G, N, G * N)
    unpack = lambda s: jnp.transpose(s.reshape(B, H // G, N, G, N), (0, 1, 3, 4, 2)).reshape(B, H, N, N)
    seq = pl.BlockSpec((None, tblk, D), lambda bi, ti: (bi, ti, 0))
    st = pl.BlockSpec((None, H // G, N, G * N), lambda bi, ti: (bi, 0, 0, 0))
    o, s_new = pl.pallas_call(
        functools.partial(_wkv_chunk_kernel, tblk=tblk, groups=H // G),
        grid=(B, T // tblk), in_specs=[seq] * 6 + [st], out_specs=[seq, st],
        out_shape=[jax.ShapeDtypeStruct((B, T, D), F32), jax.ShapeDtypeStruct((B, H // G, N, G * N), F32)],
        scratch_shapes=[pltpu.VMEM((H // G, N, G * N), F32)],
        compiler_params=_params(("arbitrary", "arbitrary")), name="wkv_chunked")(r, lw, k, v, kk, b, pack(s0))
    return o, unpack(s_new)


def _finish(y, wo_ref, x_ref, gpost_ref, out_ref):
    z = jnp.dot(y.astype(BF16), wo_ref[...], preferred_element_type=F32)
    out_ref[...] = x_ref[...] + _rms(z, gpost_ref[...])


def _rwkv_out_kernel(o_ref, r_ref, k_ref, v_ref, g_ref, x_ref, vec_ref, wo_ref, gpost_ref, out_ref):
    rk, lnw, lnb = (vec_ref[n:n + 1, :] for n in range(3))
    o = o_ref[...]
    dev = o - _seg_sum(o, RW_HEAD) * (1.0 / RW_HEAD)
    var = _seg_sum(dev * dev, RW_HEAD) * (1.0 / RW_HEAD)
    on = dev * lax.rsqrt(var + RW_GN_EPS) * lnw + lnb
    bonus = _seg_sum(r_ref[...] * k_ref[...] * rk, RW_HEAD) * v_ref[...]
    _finish((on + bonus) * g_ref[...], wo_ref, x_ref, gpost_ref, out_ref)


def _hgrn_out_kernel(o_ref, g_ref, x_ref, ng_ref, wo_ref, gpost_ref, out_ref):
    o = o_ref[...]
    parts = []
    for j in range(o.shape[1] // HG_HEAD):
        oj = o[:, j * HG_HEAD:(j + 1) * HG_HEAD]
        parts.append(oj * lax.rsqrt(jnp.mean(oj * oj, axis=-1, keepdims=True) + NORM_EPS))
    g = g_ref[...]
    y = jnp.concatenate(parts, axis=1) * ng_ref[...] * (g * _sigmoid(g))
    _finish(y, wo_ref, x_ref, gpost_ref, out_ref)


def _fox_out_kernel(o_ref, g_ref, x_ref, wo_ref, gpost_ref, out_ref):
    _finish(o_ref[...] * _sigmoid(g_ref[...]), wo_ref, x_ref, gpost_ref, out_ref)


def _mixer_out(body, rows, consts, *, tm, name):
    M, D = rows[0].shape
    row = pl.BlockSpec((tm, D), lambda i: (i, 0))
    return pl.pallas_call(
        body, grid=(M // tm,), in_specs=[row] * len(rows) + [_const_spec(c.shape) for c in consts],
        out_specs=row, out_shape=jax.ShapeDtypeStruct((M, D), F32),
        compiler_params=_params(("arbitrary",)), name=name)(*rows, *consts)


def _ffn_kernel(x_ref, gpre_ref, gpost_ref, wg_ref, wu_ref, wd_ref, out_ref):
    x = x_ref[...]
    h = _rms(x, gpre_ref[...]).astype(BF16)
    a = jnp.dot(h, wg_ref[...], preferred_element_type=F32)
    u = jnp.dot(h, wu_ref[...], preferred_element_type=F32)
    act = (a * _sigmoid(a) * u).astype(BF16)
    z = jnp.dot(act, wd_ref[...], preferred_element_type=F32)
    out_ref[...] = x + _rms(z, gpost_ref[...])


def _ffn(x, gpre, gpost, wg, wu, wd, *, tm):
    M, D = x.shape
    row = pl.BlockSpec((tm, D), lambda i: (i, 0))
    consts = [gpre, gpost, wg, wu, wd]
    return pl.pallas_call(
        _ffn_kernel, grid=(M // tm,), in_specs=[row] + [_const_spec(c.shape) for c in consts],
        out_specs=row, out_shape=jax.ShapeDtypeStruct((M, D), F32),
        compiler_params=_params(("arbitrary",)), name="ffn")(x, *consts)


def _hgrn_proj_kernel(x_ref, gpre_ref, lb_ref, w_ref, q_out, k_out, v_out, lf_out, g_out, *, layer):
    D = x_ref.shape[1]
    h = _rms(x_ref[...], gpre_ref[...]).astype(BF16)
    proj = lambda n: jnp.dot(h, w_ref[:, n * D:(n + 1) * D], preferred_element_type=F32)
    lbw = lb_ref[...]
    e = jnp.exp(lbw - jnp.max(lbw, axis=0, keepdims=True))
    lb = jnp.sum(e[1:layer + 1], axis=0, keepdims=True) / jnp.sum(e, axis=0, keepdims=True)
    q = proj(0)
    f = lb + (1.0 - lb) * _sigmoid(proj(1))
    q_out[...] = q * _sigmoid(q)
    k_out[...] = 1.0 - f
    lf_out[...] = jnp.log(f)
    v_out[...] = proj(2)
    g_out[...] = proj(3)


def _hgrn_proj(x, gpre, hg_lb, w_in, *, layer, tm):
    M, D = x.shape
    row = pl.BlockSpec((tm, D), lambda i: (i, 0))
    return pl.pallas_call(
        functools.partial(_hgrn_proj_kernel, layer=layer), grid=(M // tm,),
        in_specs=[row, _const_spec(gpre.shape), _const_spec(hg_lb.shape), _const_spec(w_in.shape)],
        out_specs=[row] * 5, out_shape=[jax.ShapeDtypeStruct((M, D), F32)] * 5,
        compiler_params=_params(("arbitrary",)), name="hgrn_proj")(x, gpre, hg_lb, w_in)


def _gla_kernel(q_ref, k_ref, v_ref, lf_ref, s0_ref, o_ref, sout_ref, s_scr, *, tblk, heads):
    ti = pl.program_id(1)
    C = HG_CHUNK

    @pl.when(ti == 0)
    def _():
        s_scr[...] = s0_ref[...]

    lower = _iota((C, C), 0) >= _iota((C, C), 1)
    tri = jnp.where(lower, 1.0, 0.0).astype(BF16)

    def chunk(c, carry):
        t0 = pl.multiple_of(c * C, C)
        hs = range(heads)
        each = lambda f, *cols: [f(*args) for args in zip(*cols)]
        load = lambda ref: [ref[pl.ds(t0, C), hd * HG_HEAD:(hd + 1) * HG_HEAD] for hd in hs]
        q, k, v, lf = load(q_ref), load(k_ref), load(v_ref), load(lf_ref)
        b = each(lambda x: _sel_dot_left(tri, x), lf)
        qd = each(lambda x, y: (x * jnp.exp(y)).astype(BF16), q, b)
        att = each(lambda x, y, z: jnp.where(lower, _dot_nt(x, y * jnp.exp(-z)), 0.0), qd, k, b)
        st = [s_scr[hd] for hd in hs]
        o1 = each(_dot_nt, qd, st)
        o2 = each(_bdot, att, v)
        for hd in hs:
            o_ref[pl.ds(t0, C), hd * HG_HEAD:(hd + 1) * HG_HEAD] = o1[hd] + o2[hd]
        kv = each(lambda x, y, z: _dot_tn(x, y * jnp.exp(z[C - 1:C, :] - z)), v, k, b)
        for hd in hs:
            s_scr[hd] = st[hd] * jnp.exp(b[hd][C - 1:C, :]) + kv[hd]
        return carry

    lax.fori_loop(0, tblk // C, chunk, 0)

    @pl.when(ti == pl.num_programs(1) - 1)
    def _():
        sout_ref[...] = s_scr[...]


def _gla(q, k, v, lf, s0t, *, tblk):
    B, T, D = q.shape
    H = s0t.shape[1]
    seq = pl.BlockSpec((None, tblk, D), lambda bi, ti: (bi, ti, 0))
    st = pl.BlockSpec((None, H, HG_HEAD, HG_HEAD), lambda bi, ti: (bi, 0, 0, 0))
    return pl.pallas_call(
        functools.partial(_gla_kernel, tblk=tblk, heads=H),
        grid=(B, T // tblk), in_specs=[seq] * 4 + [st], out_specs=[seq, st],
        out_shape=[jax.ShapeDtypeStruct((B, T, D), F32), jax.ShapeDtypeStruct(s0t.shape, F32)],
        scratch_shapes=[pltpu.VMEM((H, HG_HEAD, HG_HEAD), F32)],
        compiler_params=_params(("arbitrary", "arbitrary")), name="gla_chunked")(q, k, v, lf, s0t)


def _gla_step_kernel(q_ref, k_ref, v_ref, lf_ref, s0_ref, o_ref, sout_ref, *, heads):
    n = HG_HEAD
    eye = jnp.where(_iota((n, n), 0) == _iota((n, n), 1), 1.0, 0.0)
    ones = jnp.ones((n, n), BF16)
    for hd in range(heads):
        sl = slice(hd * n, (hd + 1) * n)
        vcol = _sel_dot(eye * v_ref[:, sl], ones)
        st = s0_ref[hd] * jnp.exp(lf_ref[:, sl]) + vcol * k_ref[:, sl]
        sout_ref[hd] = st
        o_ref[:, sl] = _dot_nt(jnp.broadcast_to(q_ref[:, sl], (8, n)), st)[0:1, :]


def _gla_step(q, k, v, lf, s0t):
    B, _, D = q.shape
    H = s0t.shape[1]
    seq = pl.BlockSpec((None, 1, D), lambda bi: (bi, 0, 0))
    st = pl.BlockSpec((None, H, HG_HEAD, HG_HEAD), lambda bi: (bi, 0, 0, 0))
    return pl.pallas_call(
        functools.partial(_gla_step_kernel, heads=H), grid=(B,),
        in_specs=[seq] * 4 + [st], out_specs=[seq, st],
        out_shape=[jax.ShapeDtypeStruct((B, 1, D), F32), jax.ShapeDtypeStruct(s0t.shape, F32)],
        compiler_params=_params(("arbitrary",)), name="gla_step")(q, k, v, lf, s0t)


def _fox_proj_kernel(x_ref, gpre_ref, w_ref, wf_ref, bf_ref, qn_ref, kn_ref,
                     q_out, k_out, v_out, g_out, lf_out, c_out, carry_scr, *, seq_tiles, single_step):
    D = x_ref.shape[1]
    tm = x_ref.shape[0]
    h = _rms(x_ref[...], gpre_ref[...]).astype(BF16)
    proj = lambda n: jnp.dot(h, w_ref[:, n * D:(n + 1) * D], preferred_element_type=F32)
    head_norm = lambda t, gain: t * lax.rsqrt(_seg_sum(t * t, FX_HEAD) * (1.0 / FX_HEAD) + NORM_EPS) * gain
    q_out[...] = head_norm(proj(0), qn_ref[...]) * (FX_HEAD ** -0.5)
    k_out[...] = head_norm(proj(1), kn_ref[...])
    v_out[...] = proj(2)
    g_out[...] = proj(3)
    z = jnp.dot(h, wf_ref[...], preferred_element_type=F32) + bf_ref[...]
    lf = jnp.minimum(z, 0.0) - jnp.log1p(jnp.exp(-jnp.abs(z)))
    lf_out[...] = lf[:, :FX_HEADS]
    if single_step:
        c_out[...] = lf[:, :FX_HEADS]
    else:
        ti = pl.program_id(0) % seq_tiles

        @pl.when(ti == 0)
        def _():
            carry_scr[...] = jnp.zeros_like(carry_scr)

        tri = jnp.where(_iota((tm, tm), 0) >= _iota((tm, tm), 1), 1.0, 0.0).astype(BF16)
        c = _sel_dot_left(tri, lf) + carry_scr[...]
        carry_scr[...] = c[tm - 1:tm, :]
        c_out[...] = c[:, :FX_HEADS]


def _fox_proj(x, gpre, w_main, w_f, b_f, qn, kn, *, seq_len, tm):
    M, D = x.shape
    single_step = seq_len == 1
    row = pl.BlockSpec((tm, D), lambda i: (i, 0))
    hrow = pl.BlockSpec((tm, FX_HEADS), lambda i: (i, 0))
    consts = [gpre, w_main, w_f, b_f, qn, kn]
    return pl.pallas_call(
        functools.partial(_fox_proj_kernel, seq_tiles=1 if single_step else seq_len // tm,
                          single_step=single_step),
        grid=(M // tm,), in_specs=[row] + [_const_spec(c.shape) for c in consts],
        out_specs=[row] * 4 + [hrow] * 2,
        out_shape=[jax.ShapeDtypeStruct((M, D), F32)] * 4 + [jax.ShapeDtypeStruct((M, FX_HEADS), F32)] * 2,
        scratch_shapes=[pltpu.VMEM((1, LANES), F32)],
        compiler_params=_params(("arbitrary",)), name="fox_proj")(x, *consts)


def _fox_attn_kernel(q_ref, k_ref, vt_ref, cq_ref, ck_ref, o_ref, *, tq, tk):
    qi = pl.program_id(2)
    n = FX_HEAD
    first = _iota((1, LANES), 1) < n
    q = q_ref[...] * LOG2E
    q0 = jnp.where(first, q, 0.0).astype(BF16)
    q1 = jnp.where(first, 0.0, q).astype(BF16)
    cq = cq_ref[...] * LOG2E
    cq0, cq1 = cq[0:1, :], cq[1:2, :]
    full_blocks = (qi * tq) // tk

    def block(ki, carry, diagonal):
        m0, l0, acc0, m1, l1, acc1 = carry
        k0 = pl.multiple_of(ki * tk, tk)
        kb = k_ref[pl.ds(k0, tk), :].astype(BF16)
        vt = vt_ref[ki].astype(BF16)
        ck = ck_ref[pl.ds(k0, tk), :] * LOG2E
        keep = None
        if diagonal:
            keep = (_iota((tk, tq), 0) - _iota((tk, tq), 1)) <= (qi * tq - k0)

        def softmax(s, cqh, ckh, m, l):
            s = s + cqh - ckh
            if diagonal:
                s = jnp.where(keep, s, NEG_BIG)
            m_new = jnp.maximum(m, jnp.max(s, axis=0, keepdims=True))
            p = jnp.exp2(s - m_new)
            a = jnp.exp2(m - m_new)
            return m_new, l * a + jnp.sum(p, axis=0, keepdims=True), a, p.astype(BF16)

        s0, s1 = _dot_nt(kb, q0), _dot_nt(kb, q1)
        m0, l0, a0, p0 = softmax(s0, cq0, ck[:, 0:1], m0, l0)
        m1, l1, a1, p1 = softmax(s1, cq1, ck[:, 1:2], m1, l1)
        pv0 = jnp.dot(vt[:n], p0, preferred_element_type=F32)
        pv1 = jnp.dot(vt[n:], p1, preferred_element_type=F32)
        return m0, l0, acc0 * a0 + pv0, m1, l1, acc1 * a1 + pv1

    row = lambda val: jnp.full((1, tq), val, F32)
    init = (row(NEG_BIG), row(0.0), jnp.zeros((n, tq), F32)) * 2
    carry = lax.fori_loop(0, full_blocks, lambda ki, c: block(ki, c, False), init)
    _, l0, acc0, _, l1, acc1 = block(full_blocks, carry, True)
    o_ref[...] = jnp.concatenate([acc0 / l0, acc1 / l1], axis=0).T


def _fox_attn(q, k, v, c, *, tq, tk):
    B, T, D = q.shape
    hp = D // LANES
    ck = jnp.transpose(c.reshape(B, T, hp, 2), (0, 2, 1, 3))
    cq = jnp.transpose(c.reshape(B, T, hp, 2), (0, 2, 3, 1))
    vt = jnp.transpose(v.reshape(B, T // tk, tk, hp, LANES), (0, 3, 1, 4, 2))
    qs = pl.BlockSpec((None, tq, LANES), lambda b, p, i: (b, i, p))
    ks = pl.BlockSpec((None, T, LANES), lambda b, p, i: (b, 0, p))
    vs = pl.BlockSpec((None, None, T // tk, LANES, tk), lambda b, p, i: (b, p, 0, 0, 0))
    cqs = pl.BlockSpec((None, None, 2, tq), lambda b, p, i: (b, p, 0, i))
    cks = pl.BlockSpec((None, None, T, 2), lambda b, p, i: (b, p, 0, 0))
    return pl.pallas_call(
        functools.partial(_fox_attn_kernel, tq=tq, tk=tk), grid=(B, hp, T // tq),
        in_specs=[qs, ks, vs, cqs, cks], out_specs=qs, out_shape=jax.ShapeDtypeStruct((B, T, D), F32),
        compiler_params=_params(("arbitrary",) * 3), name="fox_attn")(q, k, vt, cq, ck)


def _fox_paged_kernel(*refs, npg):
    it = iter(refs)
    _pt_ref = next(it)
    k_refs, v_refs, suf_refs, tot_refs = ([next(it) for _ in range(npg)] for _ in range(4))
    q_ref, kn_ref, vn_ref, cq_ref, o_ref, m_scr, l_scr, carry_scr, acc_scr = (next(it) for _ in range(9))
    g = pl.program_id(1)
    H, P = FX_HEADS, PAGE_SIZE
    R = P * H

    @pl.when(g == 0)
    def _():
        m_scr[...] = jnp.full_like(m_scr, NEG_BIG)
        l_scr[...] = jnp.zeros_like(l_scr)
        carry_scr[...] = jnp.zeros_like(carry_scr)
        acc_scr[...] = jnp.zeros_like(acc_scr)

    own = _iota((H, R), 0) == _iota((H, R), 1) % H
    per_token = lambda x: jnp.concatenate([x] * (R // LANES), axis=1)
    qb = q_ref[...].astype(BF16)
    cq = cq_ref[...]
    carry = carry_scr[...]
    scores = [None] * npg
    for j in reversed(range(npg)):
        bias = suf_refs[j][...] + per_token(carry + cq)
        scores[j] = jnp.where(own, _dot_nt(qb, k_refs[j][...]) + bias, NEG_BIG)
        carry = carry + tot_refs[j][...]
    carry_scr[...] = carry
    m_old = m_scr[...]
    m_new = m_old
    for j in range(npg):
        m_new = jnp.maximum(m_new, jnp.max(scores[j], axis=1, keepdims=True))
    alpha = jnp.exp(m_old - m_new)
    l_new = l_scr[...] * alpha
    acc = acc_scr[...] * alpha
    for j in range(npg):
        p = jnp.exp(scores[j] - m_new)
        l_new = l_new + jnp.sum(p, axis=1, keepdims=True)
        acc = acc + _bdot(p, v_refs[j][...])
    acc_scr[...] = acc
    m_scr[...] = m_new
    l_scr[...] = l_new

    @pl.when(g == pl.num_programs(1) - 1)
    def _():
        s_new = jnp.sum(q_ref[...] * kn_ref[...], axis=1, keepdims=True)
        m_fin = jnp.maximum(m_new, s_new)
        a_fin = jnp.exp(m_new - m_fin)
        p_new = jnp.exp(s_new - m_fin)
        o_ref[...] = (acc * a_fin + vn_ref[...] * p_new) / (l_new * a_fin + p_new)


def _fox_page_bias_kernel(lf_ref, suf_ref, tot_ref):
    P = lf_ref.shape[1]
    later = jnp.where(_iota((P, P), 1) > _iota((P, P), 0), 1.0, 0.0).astype(BF16)
    for i in range(lf_ref.shape[0]):
        lf = lf_ref[i]
        suf_ref[i] = _sel_dot_left(later, lf)
        tot_ref[i] = jnp.sum(lf, axis=0, keepdims=True)


def _fox_page_bias(cache_lf, *, pages_per_step):
    n, P, H = cache_lf.shape
    blk = pl.BlockSpec((pages_per_step, P, H), lambda i: (i, 0, 0))
    tot = pl.BlockSpec((pages_per_step, 1, H), lambda i: (i, 0, 0))
    return pl.pallas_call(
        _fox_page_bias_kernel, grid=(n // pages_per_step,), in_specs=[blk], out_specs=[blk, tot],
        out_shape=[jax.ShapeDtypeStruct((n, P, H), F32), jax.ShapeDtypeStruct((n, 1, H), F32)],
        compiler_params=_params(("arbitrary",)), name="fox_page_bias")(cache_lf)


def _fox_paged(q, k_new, v_new, lf_new, cache_k, cache_v, cache_lf, page_table, *, npg):
    B, _, D = q.shape
    H, N, P = FX_HEADS, FX_HEAD, PAGE_SIZE
    R = P * H
    n_pages = page_table.shape[1]
    groups = n_pages // npg
    suf, tot = _fox_page_bias(cache_lf, pages_per_step=math.gcd(cache_lf.shape[0], 32))
    suf = suf.reshape(-1, 1, R)
    tot = jnp.tile(tot, (1, 1, LANES // H))
    cq = jnp.tile(lf_new, (1, 1, LANES // H))
    heads = lambda t: t.reshape(B, H, N)

    def page(j):
        return lambda b, g, pt: (pt[b, (groups - 1 - g) * npg + j], 0, 0)

    kspecs = [pl.BlockSpec((None, R, N), page(j)) for j in range(npg)]
    sspecs = [pl.BlockSpec((None, 1, R), page(j)) for j in range(npg)]
    tspecs = [pl.BlockSpec((None, 1, LANES), page(j)) for j in range(npg)]
    tok = pl.BlockSpec((None, H, N), lambda b, g, pt: (b, 0, 0))
    tokh = pl.BlockSpec((None, 1, LANES), lambda b, g, pt: (b, 0, 0))
    grid_spec = pltpu.PrefetchScalarGridSpec(
        num_scalar_prefetch=1, grid=(B, groups),
        in_specs=kspecs + kspecs + sspecs + tspecs + [tok, tok, tok, tokh], out_specs=tok,
        scratch_shapes=[pltpu.VMEM((H, 1), F32), pltpu.VMEM((H, 1), F32), pltpu.VMEM((1, LANES), F32),
                        pltpu.VMEM((H, N), F32)])
    out = pl.pallas_call(
        functools.partial(_fox_paged_kernel, npg=npg), grid_spec=grid_spec,
        out_shape=jax.ShapeDtypeStruct((B, H, N), F32),
        compiler_params=_params(("arbitrary", "arbitrary")), name="fox_paged")(
            page_table, *([cache_k] * npg), *([cache_v] * npg), *([suf] * npg), *([tot] * npg),
            heads(q), heads(k_new), heads(v_new), cq)
    return out.reshape(B, 1, D)


def _row_tile(m, want):
    return want if m % want == 0 else m


def kernel(x_prompt, x_sample, state_rwkv_wkv, state_rwkv_shift, state_hgrn, cache_fox_k, cache_fox_v, cache_fox_logf, page_table, norm_pre_mix, norm_post_mix, norm_pre_ffn, norm_post_ffn, rw_mu, rw_wr, rw_wk, rw_wv, rw_wo, rw_w0, rw_w1, rw_w2, rw_a0, rw_a1, rw_a2, rw_v0, rw_v1, rw_v2, rw_g1, rw_g2, rw_kk, rw_ka, rw_rk, rw_lnw, rw_lnb, hg_w_in, hg_lb, hg_norm, hg_wo, fx_w_in, fx_bf, fx_qn, fx_kn, fx_wo, ffn_wg, ffn_wu, ffn_wd):
    D = D_MODEL
    bf = lambda t: t.astype(BF16)
    vec = lambda t: t.reshape(1, -1).astype(F32)

    n_cache_pages = cache_fox_k.shape[1]
    cache_k = cache_fox_k.reshape(-1, PAGE_SIZE * FX_HEADS, FX_HEAD)
    cache_v = cache_fox_v.reshape(-1, PAGE_SIZE * FX_HEADS, FX_HEAD)
    cache_lf = cache_fox_logf.reshape(-1, PAGE_SIZE, FX_HEADS)

    def trunk(x3, wkv0, shift0, hg0, paged):
        B, T, _ = x3.shape
        M = B * T
        x = x3.reshape(M, D)
        tm = _row_tile(M, 256)
        v_first = None
        wkv_n, shift_n, hg_n, k_n, v_n, lf_n = [], [], [], [], [], []
        for i in range(DEPTH):
            j = i // N_MIXERS
            gpre, gpost = vec(norm_pre_mix[i]), vec(norm_post_mix[i])
            if i % N_MIXERS == 0:
                vecs = [rw_w0[j], rw_a0[j], rw_kk[j], rw_ka[j]]
                vecs += [rw_v0[j - 1]] if j > 0 else []
                vecs = jnp.stack(vecs + [jnp.zeros((D,), F32)] * (8 - len(vecs)))
                mats = [bf(t[j]) for t in (rw_wr, rw_wk, rw_wv, rw_w1, rw_w2, rw_a1, rw_a2, rw_g1, rw_g2)]
                vres = None if j == 0 else (bf(rw_v1[j - 1]), bf(rw_v2[j - 1]), v_first)
                r, w, k, v, kk, kb, g, shift = _rwkv_proj(x, shift0[j], gpre, rw_mu[j], vecs, mats, vres,
                                                          seq_len=T, tm=tm)
                if j == 0:
                    v_first = v
                seq = lambda t: t.reshape(B, T, D)
                wkv = _wkv_step if T == 1 else functools.partial(_wkv_chunked, tblk=min(T, 256))
                o, s_new = wkv(seq(r), seq(w), seq(k), seq(v), seq(kk), seq(kb), wkv0[j])
                ovec = jnp.stack([rw_rk[j].reshape(-1), rw_lnw[j], rw_lnb[j]] + [jnp.zeros((D,), F32)] * 5)
                x = _mixer_out(_rwkv_out_kernel, [o.reshape(M, D), r, k, v, g, x],
                               [ovec, bf(rw_wo[j]), gpost], tm=tm, name="rwkv_out")
                wkv_n.append(s_new)
                shift_n.append(shift)
            elif i % N_MIXERS == 1:
                q, k, v, lf, g = _hgrn_proj(x, gpre, hg_lb, bf(hg_w_in[j]), layer=i, tm=tm)
                seq = lambda t: t.reshape(B, T, D)
                s0t = jnp.swapaxes(hg0[j], -1, -2)
                if T == 1:
                    o, st = _gla_step(seq(q), seq(k), seq(v), seq(lf), s0t)
                else:
                    o, st = _gla(seq(q), seq(k), seq(v), seq(lf), s0t, tblk=min(T, 256))
                x = _mixer_out(_hgrn_out_kernel, [o.reshape(M, D), g, x],
                               [vec(hg_norm[j]), bf(hg_wo[j]), gpost], tm=tm, name="hgrn_out")
                hg_n.append(jnp.swapaxes(st, -1, -2))
            else:
                w_in = fx_w_in[j]
                w_f = jnp.pad(w_in[:, 4 * D:], ((0, 0), (0, LANES - FX_HEADS)))
                b_f = jnp.pad(fx_bf[j], (0, LANES - FX_HEADS)).reshape(1, LANES)
                tile = lambda t: jnp.tile(t, FX_HEADS).reshape(1, D)
                q, k, v, g, lf, c = _fox_proj(x, gpre, bf(w_in[:, :4 * D]), bf(w_f), b_f,
                                              tile(fx_qn[j]), tile(fx_kn[j]), seq_len=T, tm=tm)
                seq = lambda t: t.reshape(B, T, D)
                if paged:
                    pt = page_table + j * n_cache_pages
                    att = _fox_paged(seq(q), seq(k), seq(v), lf.reshape(B, 1, FX_HEADS),
                                     cache_k, cache_v, cache_lf, pt, npg=4)
                else:
                    att = _fox_attn(seq(q), seq(k), seq(v), c.reshape(B, T, FX_HEADS), tq=256, tk=512)
                x = _mixer_out(_fox_out_kernel, [att.reshape(M, D), g, x], [bf(fx_wo[j]), gpost],
                               tm=tm, name="fox_out")
                k_n.append(k.reshape(B, T, FX_HEADS, FX_HEAD))
                v_n.append(v.reshape(B, T, FX_HEADS, FX_HEAD))
                lf_n.append(lf.reshape(B, T, FX_HEADS))
            x = _ffn(x, vec(norm_pre_ffn[i]), vec(norm_post_ffn[i]), bf(ffn_wg[i]), bf(ffn_wu[i]),
                     bf(ffn_wd[i]), tm=tm)
        return (x.reshape(B, T, D), jnp.stack(wkv_n), jnp.stack(shift_n), jnp.stack(hg_n),
                jnp.stack(k_n), jnp.stack(v_n), jnp.stack(lf_n))

    bp = x_prompt.shape[0]
    n_a, n_b = state_rwkv_wkv.shape[0], state_hgrn.shape[0]
    y_p, wkv_p, shift_p, hg_p, k_p, v_p, lf_p = trunk(
        x_prompt, jnp.zeros((n_a, bp) + state_rwkv_wkv.shape[2:], F32), jnp.zeros((n_a, bp, D), F32),
        jnp.zeros((n_b, bp) + state_hgrn.shape[2:], F32), False)
    y_s, wkv_s, shift_s, hg_s, k_s, v_s, lf_s = trunk(
        x_sample, state_rwkv_wkv, state_rwkv_shift, state_hgrn, True)
    return (y_p, y_s, wkv_p, wkv_s, shift_p, shift_s, hg_p, hg_s, k_p, k_s, v_p, v_s, lf_p, lf_s)
```

```python
import functools
import math

import jax
import jax.numpy as jnp
from jax import lax
from jax.experimental import pallas as pl
from jax.experimental.pallas import tpu as pltpu

F32 = jnp.float32
BF16 = jnp.bfloat16

LANES = 128
VMEM_LIMIT = 56 * 1024 * 1024

D_MODEL = 1024
DEPTH = 4
N_MIXERS = 3
RW_HEAD = 64
RW_HEADS = D_MODEL // RW_HEAD
RW_DECAY_SCALE = 0.6065306597126334
RW_GN_EPS = 64e-5
HG_HEADS = 8
HG_HEAD = D_MODEL // HG_HEADS
HG_CHUNK = 16
FX_HEAD = 64
FX_HEADS = D_MODEL // FX_HEAD
PAGE_SIZE = 128
D_FF = 2816
NORM_EPS = 1e-6
NEG_BIG = -1e30
LOG2E = 1.4426950408889634


def _params(sem):
    return pltpu.CompilerParams(dimension_semantics=sem, vmem_limit_bytes=VMEM_LIMIT)


def _const_spec(shape):
    n = len(shape)
    return pl.BlockSpec(shape, lambda *_: (0,) * n, pipeline_mode=pl.Buffered(1))


def _rms(x, g, eps=NORM_EPS):
    return x * lax.rsqrt(jnp.mean(x * x, axis=-1, keepdims=True) + eps) * g


def _sigmoid(x):
    return 1.0 / (1.0 + jnp.exp(-x))


def _bdot(a, b):
    return jnp.dot(a.astype(BF16), b.astype(BF16), preferred_element_type=F32)


def _dot_nt(a, b):
    return lax.dot_general(a.astype(BF16), b.astype(BF16), (((1,), (1,)), ((), ())),
                           preferred_element_type=F32)


def _dot_tn(a, b):
    return lax.dot_general(a.astype(BF16), b.astype(BF16), (((0,), (0,)), ((), ())),
                           preferred_element_type=F32)


def _split(x):
    hi = x.astype(BF16)
    lo = (x - hi.astype(F32)).astype(BF16)
    return hi, lo


def _sel_dot(x, sel):
    hi, lo = _split(x)
    return (jnp.dot(hi, sel, preferred_element_type=F32)
            + jnp.dot(lo, sel, preferred_element_type=F32))


def _sel_dot_left(sel, x):
    hi, lo = _split(x)
    return (jnp.dot(sel, hi, preferred_element_type=F32)
            + jnp.dot(sel, lo, preferred_element_type=F32))


def _iota(shape, dim):
    return lax.broadcasted_iota(jnp.int32, shape, dim)


def _block_ones(n, seg):
    same = (_iota((n, n), 0) // seg) == (_iota((n, n), 1) // seg)
    return jnp.where(same, 1.0, 0.0).astype(BF16)


def _seg_sum(x, seg):
    sel = _block_ones(LANES, seg)
    parts = [_sel_dot(x[:, j * LANES:(j + 1) * LANES], sel) for j in range(x.shape[1] // LANES)]
    return jnp.concatenate(parts, axis=1)


def _rwkv_proj_kernel(*refs, seq_tiles, has_vres, single_step):
    it = iter(refs)
    x_ref, shift_ref, gpre_ref, mu_ref, vec_ref = (next(it) for _ in range(5))
    wr_ref, wk_ref, wv_ref, w1_ref, w2_ref, a1_ref, a2_ref, g1_ref, g2_ref = (next(it) for _ in range(9))
    if has_vres:
        v1_ref, v2_ref, vfirst_ref = (next(it) for _ in range(3))
    r_out, w_out, k_out, v_out, kk_out, b_out, g_out, shift_out = (next(it) for _ in range(8))
    last_scr = next(it)

    h = _rms(x_ref[...], gpre_ref[...])
    tm = h.shape[0]
    if single_step:
        prev = shift_ref[...]
        shift_out[...] = h
    else:
        i = pl.program_id(0)
        b = i // seq_tiles
        ti = i % seq_tiles
        first = jnp.where(ti == 0, shift_ref[pl.ds(b, 1), :], last_scr[...])
        prev = jnp.where(_iota((tm, 1), 0) == 0, first, pltpu.roll(h, 1, 0))
        last_scr[...] = h[tm - 1:tm, :]

        @pl.when(ti == seq_tiles - 1)
        def _():
            shift_out[pl.ds(b, 1), :] = h[tm - 1:tm, :]

    d = prev - h
    mix = lambda n: h + d * mu_ref[n:n + 1, :]
    w0, a0, k_k, k_a = (vec_ref[n:n + 1, :] for n in range(4))

    xv = mix(3)
    r = _bdot(mix(0), wr_ref[...])
    k = _bdot(mix(2), wk_ref[...])
    v = _bdot(xv, wv_ref[...])
    wl = w0 + _bdot(jnp.tanh(_bdot(mix(1), w1_ref[...])), w2_ref[...])
    a = _sigmoid(a0 + _bdot(_bdot(mix(4), a1_ref[...]), a2_ref[...]))
    if has_vres:
        gate = _sigmoid(vec_ref[4:5, :] + _bdot(_bdot(xv, v1_ref[...]), v2_ref[...]))
        v = v + (vfirst_ref[...] - v) * gate
    g = _bdot(_sigmoid(_bdot(mix(5), g1_ref[...])), g2_ref[...])

    kk = k * k_k
    kk = kk / jnp.maximum(jnp.sqrt(_seg_sum(kk * kk, RW_HEAD)), 1e-12)
    r_out[...] = r
    w_out[...] = -RW_DECAY_SCALE * _sigmoid(wl)
    k_out[...] = k * (1.0 + (a - 1.0) * k_a)
    v_out[...] = v
    kk_out[...] = kk
    b_out[...] = kk * a
    g_out[...] = g


def _rwkv_proj(x, shift0, gpre, mu, vecs, mats, vres, *, seq_len, tm):
    M, D = x.shape
    B = M // seq_len
    single_step = seq_len == 1
    seq_tiles = 1 if single_step else seq_len // tm
    row = pl.BlockSpec((tm, D), lambda i: (i, 0))
    shift_spec = row if single_step else _const_spec((B, D))
    shift_out_spec = row if single_step else pl.BlockSpec((B, D), lambda i: (0, 0))
    ins = [x, shift0, gpre, mu, vecs] + list(mats)
    specs = [row, shift_spec, _const_spec(gpre.shape), _const_spec(mu.shape), _const_spec(vecs.shape)]
    specs += [_const_spec(m.shape) for m in mats]
    if vres is not None:
        v1, v2, vfirst = vres
        ins += [v1, v2, vfirst]
        specs += [_const_spec(v1.shape), _const_spec(v2.shape), row]
    outs = [jax.ShapeDtypeStruct((M, D), F32)] * 7 + [jax.ShapeDtypeStruct((B, D), F32)]
    return pl.pallas_call(
        functools.partial(_rwkv_proj_kernel, seq_tiles=seq_tiles, has_vres=vres is not None,
                          single_step=single_step),
        grid=(M // tm,), in_specs=specs, out_specs=[row] * 7 + [shift_out_spec], out_shape=outs,
        scratch_shapes=[pltpu.VMEM((1, D), F32)],
        compiler_params=_params(("arbitrary",)), name="rwkv_proj")(*ins)


def _wkv_step_kernel(r_ref, lw_ref, k_ref, v_ref, kk_ref, b_ref, s0_ref, o_ref, sout_ref, *, pairs):
    n = RW_HEAD
    ones2 = _block_ones(LANES, n)
    diag2 = jnp.where(_iota((n, LANES), 0) == (_iota((n, LANES), 1) % n), 1.0, 0.0)
    for p in range(pairs):
        sl = slice(p * LANES, (p + 1) * LANES)
        s = jnp.concatenate([s0_ref[2 * p], s0_ref[2 * p + 1]], axis=1)
        lhs = jnp.concatenate([s * kk_ref[:, sl], diag2 * v_ref[:, sl]], axis=0)
        res = jnp.dot(lhs.astype(BF16), ones2, preferred_element_type=F32)
        s = s * jnp.exp(lw_ref[:, sl]) - res[:n] * b_ref[:, sl] + res[n:] * k_ref[:, sl]
        ob = jnp.dot((s * r_ref[:, sl]).astype(BF16), ones2, preferred_element_type=F32)
        o_ref[:, sl] = jnp.sum(ob * diag2, axis=0, keepdims=True)
        sout_ref[2 * p] = s[:, :n]
        sout_ref[2 * p + 1] = s[:, n:]


def _wkv_step(r, lw, k, v, kk, b, s0):
    B, _, D = r.shape
    H = s0.shape[1]
    seq = pl.BlockSpec((None, 1, D), lambda bi: (bi, 0, 0))
    st = pl.BlockSpec((None, H, RW_HEAD, RW_HEAD), lambda bi: (bi, 0, 0, 0))
    return pl.pallas_call(
        functools.partial(_wkv_step_kernel, pairs=H // 2), grid=(B,),
        in_specs=[seq] * 6 + [st], out_specs=[seq, st],
        out_shape=[jax.ShapeDtypeStruct((B, 1, D), F32), jax.ShapeDtypeStruct(s0.shape, F32)],
        compiler_params=_params(("arbitrary",)), name="wkv_step")(r, lw, k, v, kk, b, s0)


WKV_CHUNK = 64
WKV_GROUP = 256 // RW_HEAD


def _wkv_chunk_kernel(r_ref, lw_ref, k_ref, v_ref, kk_ref, b_ref, s0_ref, o_ref, sout_ref, s_scr, *, tblk, groups):
    C, W = WKV_CHUNK, 256
    ti = pl.program_id(1)

    @pl.when(ti == 0)
    def _():
        s_scr[...] = s0_ref[...]

    blk = jnp.where((_iota((W, W), 0) // C) == (_iota((W, W), 1) // C), 1.0, 0.0)
    blkb = blk.astype(BF16)
    tcol, trow = _iota((C, W), 1) % C, _iota((C, W), 0)
    strict, incl = tcol < trow, tcol <= trow
    eye = jnp.where(tcol == trow, 1.0, 0.0)
    tri = jnp.where(_iota((C, C), 1) <= _iota((C, C), 0), 1.0, 0.0).astype(BF16)

    def bd(x):
        return jnp.concatenate([x.astype(BF16)] * (W // C), axis=0) * blkb

    def sbs(y):
        ym = y * blk
        return (ym[0:C] + ym[C:2 * C]) + (ym[2 * C:3 * C] + ym[3 * C:4 * C])

    mm = lambda a, b: jnp.dot(a.astype(BF16), b, preferred_element_type=F32)

    def chunk(c, carry):
        t0 = pl.multiple_of(c * C, C)
        G = range(groups)
        each = lambda f, *cols: [f(*args) for args in zip(*cols)]
        load = lambda ref: [ref[pl.ds(t0, C), g * W:(g + 1) * W] for g in G]
        r, lw, k, v, kk, b = (load(ref) for ref in (r_ref, lw_ref, k_ref, v_ref, kk_ref, b_ref))
        cum = each(lambda x: _sel_dot_left(tri, x), lw)
        tot = each(lambda x: x[C - 1:C, :], cum)
        e_neg = each(lambda x: jnp.exp(-x), cum)
        rt = each(lambda x, y: x * jnp.exp(y), r, cum)
        at = each(lambda x, y, z: x * jnp.exp(y - z), kk, cum, lw)
        lhs2 = each(lambda x, y: jnp.concatenate([x, y], axis=0).astype(BF16), at, rt)
        gb = each(lambda l, x, e: _dot_nt(l, bd(x * e)), lhs2, b, e_neg)
        gk = each(lambda l, x, e: _dot_nt(l, bd(x * e)), lhs2, k, e_neg)
        aab = each(lambda x: jnp.where(strict, x[:C], 0.0), gb)
        arb = each(lambda x: jnp.where(incl, x[C:], 0.0).astype(BF16), gb)
        a2k = each(lambda x: jnp.concatenate([jnp.where(strict, x[:C], 0.0), jnp.where(incl, x[C:], 0.0)], axis=0), gk)
        av = each(lambda x, y: mm(x, bd(y)), a2k, v)
        x = each(lambda n: eye - n, aab)
        pw = aab
        pbd = each(bd, pw)
        for _ in range(5):
            pw = each(mm, pw, pbd)
            pbd = each(bd, pw)
            x = each(lambda y, z: y + mm(y, z), x, pbd)
        xb = each(lambda y: y.astype(BF16), x)
        p = each(lambda y, z: mm(y, bd(z)), xb, at)
        q = each(lambda y, z: mm(y, bd(z[:C])), xb, av)
        rp = each(lambda y, a, z: y - mm(a, bd(z)), rt, arb, p)
        op = each(lambda y, a, z: y[C:] - mm(a, bd(z)), av, arb, q)
        e_end = each(lambda t, y: jnp.exp(t - y), tot, cum)
        kp = each(lambda y, e: y * e, k, e_end)
        bp = each(lambda y, e: y * e, b, e_end)
        m_s = each(lambda t, y, z: eye * jnp.exp(t) - sbs(_dot_tn(y, z)), tot, bp, p)
        n_s = each(lambda y, z, vv, qq: sbs(_dot_tn(jnp.concatenate([y, z], axis=0),
                                                    jnp.concatenate([vv, -qq], axis=0))), kp, bp, v, q)
        sbd = [bd(s_scr[g]) for g in G]
        for g in G:
            o_ref[pl.ds(t0, C), g * W:(g + 1) * W] = mm(rp[g], sbd[g]) + op[g]
        for g in G:
            s_scr[g] = mm(m_s[g], sbd[g]) + n_s[g]
        return carry

    lax.fori_loop(0, tblk // C, chunk, 0)

    @pl.when(ti == pl.num_programs(1) - 1)
    def _():
        sout_ref[...] = s_scr[...]


def _wkv_chunked(r, lw, k, v, kk, b, s0, *, tblk):
    B, T, D = r.shape
    H, N = s0.shape[1], s0.shape[2]
    G = WKV_GROUP
    pack = lambda s: jnp.transpose(s.reshape(B, H // G, G, N, N), (0, 1, 4, 2, 3)).reshape(B, H // G, N, G * N)
    unpack = lambda s: jnp.transpose(s.reshape(B, H // G, N, G, N), (0, 1, 3, 4, 2)).reshape(B, H, N, N)
    seq = pl.BlockSpec((None, tblk, D), lambda bi, ti: (bi, ti, 0))
    st = pl.BlockSpec((None, H // G, N, G * N), lambda bi, ti: (bi, 0, 0, 0))
    o, s_new = pl.pallas_call(
        functools.partial(_wkv_chunk_kernel, tblk=tblk, groups=H // G),
        grid=(B, T // tblk), in_specs=[seq] * 6 + [st], out_specs=[seq, st],
        out_shape=[jax.ShapeDtypeStruct((B, T, D), F32), jax.ShapeDtypeStruct((B, H // G, N, G * N), F32)],
        scratch_shapes=[pltpu.VMEM((H // G, N, G * N), F32)],
        compiler_params=_params(("arbitrary", "arbitrary")), name="wkv_chunked")(r, lw, k, v, kk, b, pack(s0))
    return o, unpack(s_new)


def _finish(y, wo_ref, x_ref, gpost_ref, out_ref):
    z = jnp.dot(y.astype(BF16), wo_ref[...], preferred_element_type=F32)
    out_ref[...] = x_ref[...] + _rms(z, gpost_ref[...])


def _rwkv_out_kernel(o_ref, r_ref, k_ref, v_ref, g_ref, x_ref, vec_ref, wo_ref, gpost_ref, out_ref):
    rk, lnw, lnb = (vec_ref[n:n + 1, :] for n in range(3))
    o = o_ref[...]
    dev = o - _seg_sum(o, RW_HEAD) * (1.0 / RW_HEAD)
    var = _seg_sum(dev * dev, RW_HEAD) * (1.0 / RW_HEAD)
    on = dev * lax.rsqrt(var + RW_GN_EPS) * lnw + lnb
    bonus = _seg_sum(r_ref[...] * k_ref[...] * rk, RW_HEAD) * v_ref[...]
    _finish((on + bonus) * g_ref[...], wo_ref, x_ref, gpost_ref, out_ref)


def _hgrn_out_kernel(o_ref, g_ref, x_ref, ng_ref, wo_ref, gpost_ref, out_ref):
    o = o_ref[...]
    parts = []
    for j in range(o.shape[1] // HG_HEAD):
        oj = o[:, j * HG_HEAD:(j + 1) * HG_HEAD]
        parts.append(oj * lax.rsqrt(jnp.mean(oj * oj, axis=-1, keepdims=True) + NORM_EPS))
    g = g_ref[...]
    y = jnp.concatenate(parts, axis=1) * ng_ref[...] * (g * _sigmoid(g))
    _finish(y, wo_ref, x_ref, gpost_ref, out_ref)


def _fox_out_kernel(o_ref, g_ref, x_ref, wo_ref, gpost_ref, out_ref):
    _finish(o_ref[...] * _sigmoid(g_ref[...]), wo_ref, x_ref, gpost_ref, out_ref)


def _mixer_out(body, rows, consts, *, tm, name):
    M, D = rows[0].shape
    row = pl.BlockSpec((tm, D), lambda i: (i, 0))
    return pl.pallas_call(
        body, grid=(M // tm,), in_specs=[row] * len(rows) + [_const_spec(c.shape) for c in consts],
        out_specs=row, out_shape=jax.ShapeDtypeStruct((M, D), F32),
        compiler_params=_params(("arbitrary",)), name=name)(*rows, *consts)


def _ffn_kernel(x_ref, gpre_ref, gpost_ref, wg_ref, wu_ref, wd_ref, out_ref):
    x = x_ref[...]
    h = _rms(x, gpre_ref[...]).astype(BF16)
    a = jnp.dot(h, wg_ref[...], preferred_element_type=F32)
    u = jnp.dot(h, wu_ref[...], preferred_element_type=F32)
    act = (a * _sigmoid(a) * u).astype(BF16)
    z = jnp.dot(act, wd_ref[...], preferred_element_type=F32)
    out_ref[...] = x + _rms(z, gpost_ref[...])


def _ffn(x, gpre, gpost, wg, wu, wd, *, tm):
    M, D = x.shape
    row = pl.BlockSpec((tm, D), lambda i: (i, 0))
    consts = [gpre, gpost, wg, wu, wd]
    return pl.pallas_call(
        _ffn_kernel, grid=(M // tm,), in_specs=[row] + [_const_spec(c.shape) for c in consts],
        out_specs=row, out_shape=jax.ShapeDtypeStruct((M, D), F32),
        compiler_params=_params(("arbitrary",)), name="ffn")(x, *consts)


def _hgrn_proj_kernel(x_ref, gpre_ref, lb_ref, w_ref, q_out, k_out, v_out, lf_out, g_out, *, layer):
    D = x_ref.shape[1]
    h = _rms(x_ref[...], gpre_ref[...]).astype(BF16)
    proj = lambda n: jnp.dot(h, w_ref[:, n * D:(n + 1) * D], preferred_element_type=F32)
    lbw = lb_ref[...]
    e = jnp.exp(lbw - jnp.max(lbw, axis=0, keepdims=True))
    lb = jnp.sum(e[1:layer + 1], axis=0, keepdims=True) / jnp.sum(e, axis=0, keepdims=True)
    q = proj(0)
    f = lb + (1.0 - lb) * _sigmoid(proj(1))
    q_out[...] = q * _sigmoid(q)
    k_out[...] = 1.0 - f
    lf_out[...] = jnp.log(f)
    v_out[...] = proj(2)
    g_out[...] = proj(3)


def _hgrn_proj(x, gpre, hg_lb, w_in, *, layer, tm):
    M, D = x.shape
    row = pl.BlockSpec((tm, D), lambda i: (i, 0))
    return pl.pallas_call(
        functools.partial(_hgrn_proj_kernel, layer=layer), grid=(M // tm,),
        in_specs=[row, _const_spec(gpre.shape), _const_spec(hg_lb.shape), _const_spec(w_in.shape)],
        out_specs=[row] * 5, out_shape=[jax.ShapeDtypeStruct((M, D), F32)] * 5,
        compiler_params=_params(("arbitrary",)), name="hgrn_proj")(x, gpre, hg_lb, w_in)


def _gla_kernel(q_ref, k_ref, v_ref, lf_ref, s0_ref, o_ref, sout_ref, s_scr, *, tblk, heads):
    ti = pl.program_id(1)
    C = HG_CHUNK

    @pl.when(ti == 0)
    def _():
        s_scr[...] = s0_ref[...]

    lower = _iota((C, C), 0) >= _iota((C, C), 1)
    tri = jnp.where(lower, 1.0, 0.0).astype(BF16)

    def chunk(c, carry):
        t0 = pl.multiple_of(c * C, C)
        hs = range(heads)
        each = lambda f, *cols: [f(*args) for args in zip(*cols)]
        load = lambda ref: [ref[pl.ds(t0, C), hd * HG_HEAD:(hd + 1) * HG_HEAD] for hd in hs]
        q, k, v, lf = load(q_ref), load(k_ref), load(v_ref), load(lf_ref)
        b = each(lambda x: _sel_dot_left(tri, x), lf)
        qd = each(lambda x, y: (x * jnp.exp(y)).astype(BF16), q, b)
        att = each(lambda x, y, z: jnp.where(lower, _dot_nt(x, y * jnp.exp(-z)), 0.0), qd, k, b)
        st = [s_scr[hd] for hd in hs]
        o1 = each(_dot_nt, qd, st)
        o2 = each(_bdot, att, v)
        for hd in hs:
            o_ref[pl.ds(t0, C), hd * HG_HEAD:(hd + 1) * HG_HEAD] = o1[hd] + o2[hd]
        kv = each(lambda x, y, z: _dot_tn(x, y * jnp.exp(z[C - 1:C, :] - z)), v, k, b)
        for hd in hs:
            s_scr[hd] = st[hd] * jnp.exp(b[hd][C - 1:C, :]) + kv[hd]
        return carry

    lax.fori_loop(0, tblk // C, chunk, 0)

    @pl.when(ti == pl.num_programs(1) - 1)
    def _():
        sout_ref[...] = s_scr[...]


def _gla(q, k, v, lf, s0t, *, tblk):
    B, T, D = q.shape
    H = s0t.shape[1]
    seq = pl.BlockSpec((None, tblk, D), lambda bi, ti: (bi, ti, 0))
    st = pl.BlockSpec((None, H, HG_HEAD, HG_HEAD), lambda bi, ti: (bi, 0, 0, 0))
    return pl.pallas_call(
        functools.partial(_gla_kernel, tblk=tblk, heads=H),
        grid=(B, T // tblk), in_specs=[seq] * 4 + [st], out_specs=[seq, st],
        out_shape=[jax.ShapeDtypeStruct((B, T, D), F32), jax.ShapeDtypeStruct(s0t.shape, F32)],
        scratch_shapes=[pltpu.VMEM((H, HG_HEAD, HG_HEAD), F32)],
        compiler_params=_params(("arbitrary", "arbitrary")), name="gla_chunked")(q, k, v, lf, s0t)


def _gla_step_kernel(q_ref, k_ref, v_ref, lf_ref, s0_ref, o_ref, sout_ref, *, heads):
    n = HG_HEAD
    eye = jnp.where(_iota((n, n), 0) == _iota((n, n), 1), 1.0, 0.0)
    ones = jnp.ones((n, n), BF16)
    for hd in range(heads):
        sl = slice(hd * n, (hd + 1) * n)
        vcol = _sel_dot(eye * v_ref[:, sl], ones)
        st = s0_ref[hd] * jnp.exp(lf_ref[:, sl]) + vcol * k_ref[:, sl]
        sout_ref[hd] = st
        o_ref[:, sl] = _dot_nt(jnp.broadcast_to(q_ref[:, sl], (8, n)), st)[0:1, :]


def _gla_step(q, k, v, lf, s0t):
    B, _, D = q.shape
    H = s0t.shape[1]
    seq = pl.BlockSpec((None, 1, D), lambda bi: (bi, 0, 0))
    st = pl.BlockSpec((None, H, HG_HEAD, HG_HEAD), lambda bi: (bi, 0, 0, 0))
    return pl.pallas_call(
        functools.partial(_gla_step_kernel, heads=H), grid=(B,),
        in_specs=[seq] * 4 + [st], out_specs=[seq, st],
        out_shape=[jax.ShapeDtypeStruct((B, 1, D), F32), jax.ShapeDtypeStruct(s0t.shape, F32)],
        compiler_params=_params(("arbitrary",)), name="gla_step")(q, k, v, lf, s0t)


def _fox_proj_kernel(x_ref, gpre_ref, w_ref, wf_ref, bf_ref, qn_ref, kn_ref,
                     q_out, k_out, v_out, g_out, lf_out, c_out, carry_scr, *, seq_tiles, single_step):
    D = x_ref.shape[1]
    tm = x_ref.shape[0]
    h = _rms(x_ref[...], gpre_ref[...]).astype(BF16)
    proj = lambda n: jnp.dot(h, w_ref[:, n * D:(n + 1) * D], preferred_element_type=F32)
    head_norm = lambda t, gain: t * lax.rsqrt(_seg_sum(t * t, FX_HEAD) * (1.0 / FX_HEAD) + NORM_EPS) * gain
    q_out[...] = head_norm(proj(0), qn_ref[...]) * (FX_HEAD ** -0.5)
    k_out[...] = head_norm(proj(1), kn_ref[...])
    v_out[...] = proj(2)
    g_out[...] = proj(3)
    z = jnp.dot(h, wf_ref[...], preferred_element_type=F32) + bf_ref[...]
    lf = jnp.minimum(z, 0.0) - jnp.log1p(jnp.exp(-jnp.abs(z)))
    lf_out[...] = lf[:, :FX_HEADS]
    if single_step:
        c_out[...] = lf[:, :FX_HEADS]
    else:
        ti = pl.program_id(0) % seq_tiles

        @pl.when(ti == 0)
        def _():
            carry_scr[...] = jnp.zeros_like(carry_scr)

        tri = jnp.where(_iota((tm, tm), 0) >= _iota((tm, tm), 1), 1.0, 0.0).astype(BF16)
        c = _sel_dot_left(tri, lf) + carry_scr[...]
        carry_scr[...] = c[tm - 1:tm, :]
        c_out[...] = c[:, :FX_HEADS]


def _fox_proj(x, gpre, w_main, w_f, b_f, qn, kn, *, seq_len, tm):
    M, D = x.shape
    single_step = seq_len == 1
    row = pl.BlockSpec((tm, D), lambda i: (i, 0))
    hrow = pl.BlockSpec((tm, FX_HEADS), lambda i: (i, 0))
    consts = [gpre, w_main, w_f, b_f, qn, kn]
    return pl.pallas_call(
        functools.partial(_fox_proj_kernel, seq_tiles=1 if single_step else seq_len // tm,
                          single_step=single_step),
        grid=(M // tm,), in_specs=[row] + [_const_spec(c.shape) for c in consts],
        out_specs=[row] * 4 + [hrow] * 2,
        out_shape=[jax.ShapeDtypeStruct((M, D), F32)] * 4 + [jax.ShapeDtypeStruct((M, FX_HEADS), F32)] * 2,
        scratch_shapes=[pltpu.VMEM((1, LANES), F32)],
        compiler_params=_params(("arbitrary",)), name="fox_proj")(x, *consts)


def _fox_attn_kernel(q_ref, k_ref, vt_ref, cq_ref, ck_ref, o_ref, *, tq, tk):
    qi = pl.program_id(2)
    n = FX_HEAD
    first = _iota((1, LANES), 1) < n
    q = q_ref[...] * LOG2E
    q0 = jnp.where(first, q, 0.0).astype(BF16)
    q1 = jnp.where(first, 0.0, q).astype(BF16)
    cq = cq_ref[...] * LOG2E
    cq0, cq1 = cq[0:1, :], cq[1:2, :]
    full_blocks = (qi * tq) // tk

    def block(ki, carry, diagonal):
        m0, l0, acc0, m1, l1, acc1 = carry
        k0 = pl.multiple_of(ki * tk, tk)
        kb = k_ref[pl.ds(k0, tk), :].astype(BF16)
        vt = vt_ref[ki].astype(BF16)
        ck = ck_ref[pl.ds(k0, tk), :] * LOG2E
        keep = None
        if diagonal:
            keep = (_iota((tk, tq), 0) - _iota((tk, tq), 1)) <= (qi * tq - k0)

        def softmax(s, cqh, ckh, m, l):
            s = s + cqh - ckh
            if diagonal:
                s = jnp.where(keep, s, NEG_BIG)
            m_new = jnp.maximum(m, jnp.max(s, axis=0, keepdims=True))
            p = jnp.exp2(s - m_new)
            a = jnp.exp2(m - m_new)
            return m_new, l * a + jnp.sum(p, axis=0, keepdims=True), a, p.astype(BF16)

        s0, s1 = _dot_nt(kb, q0), _dot_nt(kb, q1)
        m0, l0, a0, p0 = softmax(s0, cq0, ck[:, 0:1], m0, l0)
        m1, l1, a1, p1 = softmax(s1, cq1, ck[:, 1:2], m1, l1)
        pv0 = jnp.dot(vt[:n], p0, preferred_element_type=F32)
        pv1 = jnp.dot(vt[n:], p1, preferred_element_type=F32)
        return m0, l0, acc0 * a0 + pv0, m1, l1, acc1 * a1 + pv1

    row = lambda val: jnp.full((1, tq), val, F32)
    init = (row(NEG_BIG), row(0.0), jnp.zeros((n, tq), F32)) * 2
    carry = lax.fori_loop(0, full_blocks, lambda ki, c: block(ki, c, False), init)
    _, l0, acc0, _, l1, acc1 = block(full_blocks, carry, True)
    o_ref[...] = jnp.concatenate([acc0 / l0, acc1 / l1], axis=0).T


def _fox_attn(q, k, v, c, *, tq, tk):
    B, T, D = q.shape
    hp = D // LANES
    ck = jnp.transpose(c.reshape(B, T, hp, 2), (0, 2, 1, 3))
    cq = jnp.transpose(c.reshape(B, T, hp, 2), (0, 2, 3, 1))
    vt = jnp.transpose(v.reshape(B, T // tk, tk, hp, LANES), (0, 3, 1, 4, 2))
    qs = pl.BlockSpec((None, tq, LANES), lambda b, p, i: (b, i, p))
    ks = pl.BlockSpec((None, T, LANES), lambda b, p, i: (b, 0, p))
    vs = pl.BlockSpec((None, None, T // tk, LANES, tk), lambda b, p, i: (b, p, 0, 0, 0))
    cqs = pl.BlockSpec((None, None, 2, tq), lambda b, p, i: (b, p, 0, i))
    cks = pl.BlockSpec((None, None, T, 2), lambda b, p, i: (b, p, 0, 0))
    return pl.pallas_call(
        functools.partial(_fox_attn_kernel, tq=tq, tk=tk), grid=(B, hp, T // tq),
        in_specs=[qs, ks, vs, cqs, cks], out_specs=qs, out_shape=jax.ShapeDtypeStruct((B, T, D), F32),
        compiler_params=_params(("arbitrary",) * 3), name="fox_attn")(q, k, vt, cq, ck)


def _fox_paged_kernel(*refs, npg):
    it = iter(refs)
    _pt_ref = next(it)
    kt_refs, vt_refs, lf_refs = ([next(it) for _ in range(npg)] for _ in range(3))
    q_ref, kn_ref, vn_ref, cq_ref, o_ref, m_scr, l_scr, carry_scr, acc_scr = (next(it) for _ in range(9))
    g = pl.program_id(1)
    H, N, P = FX_HEADS, FX_HEAD, PAGE_SIZE
    D = H * N

    @pl.when(g == 0)
    def _():
        m_scr[...] = jnp.full_like(m_scr, NEG_BIG)
        l_scr[...] = jnp.zeros_like(l_scr)
        carry_scr[...] = jnp.zeros_like(carry_scr)
        acc_scr[...] = jnp.zeros_like(acc_scr)

    own = _iota((H, D), 0) == _iota((H, D), 1) // N
    qbd = jnp.where(own, q_ref[...], 0.0).astype(BF16)
    later = jnp.where(_iota((P, P), 0) > _iota((P, P), 1), 1.0, 0.0).astype(BF16)
    cq = cq_ref[...]
    carry = carry_scr[...]
    scores = [None] * npg
    for j in reversed(range(npg)):
        lf = lf_refs[j][...]
        s = jnp.dot(qbd, kt_refs[j][...].astype(BF16), preferred_element_type=F32)
        scores[j] = s + _sel_dot(lf, later) + (carry + cq)
        carry = carry + jnp.sum(lf, axis=1, keepdims=True)
    carry_scr[...] = carry
    m_old = m_scr[...]
    m_new = m_old
    for j in range(npg):
        m_new = jnp.maximum(m_new, jnp.max(scores[j], axis=1, keepdims=True))
    alpha = jnp.exp(m_old - m_new)
    l_new = l_scr[...] * alpha
    acc = acc_scr[...] * alpha
    for j in range(npg):
        p = jnp.exp(scores[j] - m_new)
        l_new = l_new + jnp.sum(p, axis=1, keepdims=True)
        acc = acc + _dot_nt(p, vt_refs[j][...])
    acc_scr[...] = acc
    m_scr[...] = m_new
    l_scr[...] = l_new

    @pl.when(g == pl.num_programs(1) - 1)
    def _():
        s_new = jnp.sum(jnp.where(own, q_ref[...] * kn_ref[...], 0.0), axis=1, keepdims=True)
        m_fin = jnp.maximum(m_new, s_new)
        a_fin = jnp.exp(m_new - m_fin)
        p_new = jnp.exp(s_new - m_fin)
        mix = (acc * a_fin + vn_ref[...] * p_new) / (l_new * a_fin + p_new)
        o_ref[...] = jnp.sum(jnp.where(own, mix, 0.0), axis=0, keepdims=True)


def _fox_paged(q, k_new, v_new, lf_new, cache_kt, cache_vt, cache_lft, page_table, *, npg):
    B, _, D = q.shape
    H, P = FX_HEADS, PAGE_SIZE
    n_pages = page_table.shape[1]
    groups = n_pages // npg

    def page(j):
        return lambda b, g, pt: (pt[b, (groups - 1 - g) * npg + j], 0, 0)

    kspecs = [pl.BlockSpec((None, D, P), page(j)) for j in range(npg)]
    lspecs = [pl.BlockSpec((None, H, P), page(j)) for j in range(npg)]
    tok = pl.BlockSpec((None, 1, D), lambda b, g, pt: (b, 0, 0))
    tokh = pl.BlockSpec((None, H, 1), lambda b, g, pt: (b, 0, 0))
    grid_spec = pltpu.PrefetchScalarGridSpec(
        num_scalar_prefetch=1, grid=(B, groups),
        in_specs=kspecs + kspecs + lspecs + [tok, tok, tok, tokh], out_specs=tok,
        scratch_shapes=[pltpu.VMEM((H, 1), F32), pltpu.VMEM((H, 1), F32), pltpu.VMEM((H, 1), F32),
                        pltpu.VMEM((H, D), F32)])
    return pl.pallas_call(
        functools.partial(_fox_paged_kernel, npg=npg), grid_spec=grid_spec,
        out_shape=jax.ShapeDtypeStruct((B, 1, D), F32),
        compiler_params=_params(("arbitrary", "arbitrary")), name="fox_paged")(
            page_table, *([cache_kt] * npg), *([cache_vt] * npg), *([cache_lft] * npg),
            q, k_new, v_new, lf_new.reshape(B, H, 1))


def _row_tile(m, want):
    return want if m % want == 0 else m


def kernel(x_prompt, x_sample, state_rwkv_wkv, state_rwkv_shift, state_hgrn, cache_fox_k, cache_fox_v, cache_fox_logf, page_table, norm_pre_mix, norm_post_mix, norm_pre_ffn, norm_post_ffn, rw_mu, rw_wr, rw_wk, rw_wv, rw_wo, rw_w0, rw_w1, rw_w2, rw_a0, rw_a1, rw_a2, rw_v0, rw_v1, rw_v2, rw_g1, rw_g2, rw_kk, rw_ka, rw_rk, rw_lnw, rw_lnb, hg_w_in, hg_lb, hg_norm, hg_wo, fx_w_in, fx_bf, fx_qn, fx_kn, fx_wo, ffn_wg, ffn_wu, ffn_wd):
    D = D_MODEL
    bf = lambda t: t.astype(BF16)
    vec = lambda t: t.reshape(1, -1).astype(F32)

    n_cache_pages = cache_fox_k.shape[1]
    cache_k = jnp.transpose(cache_fox_k, (0, 1, 3, 4, 2)).reshape(-1, D, PAGE_SIZE)
    cache_v = jnp.transpose(cache_fox_v, (0, 1, 3, 4, 2)).reshape(-1, D, PAGE_SIZE)
    cache_lf = jnp.transpose(cache_fox_logf, (0, 1, 3, 2)).reshape(-1, FX_HEADS, PAGE_SIZE)

    def trunk(x3, wkv0, shift0, hg0, paged):
        B, T, _ = x3.shape
        M = B * T
        x = x3.reshape(M, D)
        tm = _row_tile(M, 256)
        v_first = None
        wkv_n, shift_n, hg_n, k_n, v_n, lf_n = [], [], [], [], [], []
        for i in range(DEPTH):
            j = i // N_MIXERS
            gpre, gpost = vec(norm_pre_mix[i]), vec(norm_post_mix[i])
            if i % N_MIXERS == 0:
                vecs = [rw_w0[j], rw_a0[j], rw_kk[j], rw_ka[j]]
                vecs += [rw_v0[j - 1]] if j > 0 else []
                vecs = jnp.stack(vecs + [jnp.zeros((D,), F32)] * (8 - len(vecs)))
                mats = [bf(t[j]) for t in (rw_wr, rw_wk, rw_wv, rw_w1, rw_w2, rw_a1, rw_a2, rw_g1, rw_g2)]
                vres = None if j == 0 else (bf(rw_v1[j - 1]), bf(rw_v2[j - 1]), v_first)
                r, w, k, v, kk, kb, g, shift = _rwkv_proj(x, shift0[j], gpre, rw_mu[j], vecs, mats, vres,
                                                          seq_len=T, tm=tm)
                if j == 0:
                    v_first = v
                seq = lambda t: t.reshape(B, T, D)
                wkv = _wkv_step if T == 1 else functools.partial(_wkv_chunked, tblk=min(T, 256))
                o, s_new = wkv(seq(r), seq(w), seq(k), seq(v), seq(kk), seq(kb), wkv0[j])
                ovec = jnp.stack([rw_rk[j].reshape(-1), rw_lnw[j], rw_lnb[j]] + [jnp.zeros((D,), F32)] * 5)
                x = _mixer_out(_rwkv_out_kernel, [o.reshape(M, D), r, k, v, g, x],
                               [ovec, bf(rw_wo[j]), gpost], tm=tm, name="rwkv_out")
                wkv_n.append(s_new)
                shift_n.append(shift)
            elif i % N_MIXERS == 1:
                q, k, v, lf, g = _hgrn_proj(x, gpre, hg_lb, bf(hg_w_in[j]), layer=i, tm=tm)
                seq = lambda t: t.reshape(B, T, D)
                s0t = jnp.swapaxes(hg0[j], -1, -2)
                if T == 1:
                    o, st = _gla_step(seq(q), seq(k), seq(v), seq(lf), s0t)
                else:
                    o, st = _gla(seq(q), seq(k), seq(v), seq(lf), s0t, tblk=min(T, 256))
                x = _mixer_out(_hgrn_out_kernel, [o.reshape(M, D), g, x],
                               [vec(hg_norm[j]), bf(hg_wo[j]), gpost], tm=tm, name="hgrn_out")
                hg_n.append(jnp.swapaxes(st, -1, -2))
            else:
                w_in = fx_w_in[j]
                w_f = jnp.pad(w_in[:, 4 * D:], ((0, 0), (0, LANES - FX_HEADS)))
                b_f = jnp.pad(fx_bf[j], (0, LANES - FX_HEADS)).reshape(1, LANES)
                tile = lambda t: jnp.tile(t, FX_HEADS).reshape(1, D)
                q, k, v, g, lf, c = _fox_proj(x, gpre, bf(w_in[:, :4 * D]), bf(w_f), b_f,
                                              tile(fx_qn[j]), tile(fx_kn[j]), seq_len=T, tm=tm)
                seq = lambda t: t.reshape(B, T, D)
                if paged:
                    pt = page_table + j * n_cache_pages
                    att = _fox_paged(seq(q), seq(k), seq(v), lf.reshape(B, 1, FX_HEADS),
                                     cache_k, cache_v, cache_lf, pt, npg=8)
                else:
                    att = _fox_attn(seq(q), seq(k), seq(v), c.reshape(B, T, FX_HEADS), tq=256, tk=512)
                x = _mixer_out(_fox_out_kernel, [att.reshape(M, D), g, x], [bf(fx_wo[j]), gpost],
                               tm=tm, name="fox_out")
                k_n.append(k.reshape(B, T, FX_HEADS, FX_HEAD))
                v_n.append(v.reshape(B, T, FX_HEADS, FX_HEAD))
                lf_n.append(lf.reshape(B, T, FX_HEADS))
            x = _ffn(x, vec(norm_pre_ffn[i]), vec(norm_post_ffn[i]), bf(ffn_wg[i]), bf(ffn_wu[i]),
                     bf(ffn_wd[i]), tm=tm)
        return (x.reshape(B, T, D), jnp.stack(wkv_n), jnp.stack(shift_n), jnp.stack(hg_n),
                jnp.stack(k_n), jnp.stack(v_n), jnp.stack(lf_n))

    bp = x_prompt.shape[0]
    n_a, n_b = state_rwkv_wkv.shape[0], state_hgrn.shape[0]
    y_p, wkv_p, shift_p, hg_p, k_p, v_p, lf_p = trunk(
        x_prompt, jnp.zeros((n_a, bp) + state_rwkv_wkv.shape[2:], F32), jnp.zeros((n_a, bp, D), F32),
        jnp.zeros((n_b, bp) + state_hgrn.shape[2:], F32), False)
    y_s, wkv_s, shift_s, hg_s, k_s, v_s, lf_s = trunk(
        x_sample, state_rwkv_wkv, state_rwkv_shift, state_hgrn, True)
    return (y_p, y_s, wkv_p, wkv_s, shift_p, shift_s, hg_p, hg_s, k_p, k_s, v_p, v_s, lf_p, lf_s)
```

```python
import functools
import math

import jax
import jax.numpy as jnp
from jax import lax
from jax.experimental import pallas as pl
from jax.experimental.pallas import tpu as pltpu

F32 = jnp.float32
BF16 = jnp.bfloat16

LANES = 128
VMEM_LIMIT = 56 * 1024 * 1024

D_MODEL = 1024
DEPTH = 4
N_MIXERS = 3
RW_HEAD = 64
RW_HEADS = D_MODEL // RW_HEAD
RW_DECAY_SCALE = 0.6065306597126334
RW_GN_EPS = 64e-5
HG_HEADS = 8
HG_HEAD = D_MODEL // HG_HEADS
HG_CHUNK = 16
FX_HEAD = 64
FX_HEADS = D_MODEL // FX_HEAD
PAGE_SIZE = 128
D_FF = 2816
NORM_EPS = 1e-6
NEG_BIG = -1e30
LOG2E = 1.4426950408889634


def _params(sem):
    return pltpu.CompilerParams(dimension_semantics=sem, vmem_limit_bytes=VMEM_LIMIT)


def _const_spec(shape):
    n = len(shape)
    return pl.BlockSpec(shape, lambda *_: (0,) * n, pipeline_mode=pl.Buffered(1))


def _rms(x, g, eps=NORM_EPS):
    return x * lax.rsqrt(jnp.mean(x * x, axis=-1, keepdims=True) + eps) * g


def _sigmoid(x):
    return 1.0 / (1.0 + jnp.exp(-x))


def _bdot(a, b):
    return jnp.dot(a.astype(BF16), b.astype(BF16), preferred_element_type=F32)


def _dot_nt(a, b):
    return lax.dot_general(a.astype(BF16), b.astype(BF16), (((1,), (1,)), ((), ())),
                           preferred_element_type=F32)


def _dot_tn(a, b):
    return lax.dot_general(a.astype(BF16), b.astype(BF16), (((0,), (0,)), ((), ())),
                           preferred_element_type=F32)


def _split(x):
    hi = x.astype(BF16)
    lo = (x - hi.astype(F32)).astype(BF16)
    return hi, lo


def _sel_dot(x, sel):
    hi, lo = _split(x)
    return (jnp.dot(hi, sel, preferred_element_type=F32)
            + jnp.dot(lo, sel, preferred_element_type=F32))


def _sel_dot_left(sel, x):
    hi, lo = _split(x)
    return (jnp.dot(sel, hi, preferred_element_type=F32)
            + jnp.dot(sel, lo, preferred_element_type=F32))


def _iota(shape, dim):
    return lax.broadcasted_iota(jnp.int32, shape, dim)


def _block_ones(n, seg):
    same = (_iota((n, n), 0) // seg) == (_iota((n, n), 1) // seg)
    return jnp.where(same, 1.0, 0.0).astype(BF16)


def _seg_sum(x, seg):
    sel = _block_ones(LANES, seg)
    parts = [_sel_dot(x[:, j * LANES:(j + 1) * LANES], sel) for j in range(x.shape[1] // LANES)]
    return jnp.concatenate(parts, axis=1)


def _rwkv_proj_kernel(*refs, seq_tiles, has_vres, single_step):
    it = iter(refs)
    x_ref, shift_ref, gpre_ref, mu_ref, vec_ref = (next(it) for _ in range(5))
    wr_ref, wk_ref, wv_ref, w1_ref, w2_ref, a1_ref, a2_ref, g1_ref, g2_ref = (next(it) for _ in range(9))
    if has_vres:
        v1_ref, v2_ref, vfirst_ref = (next(it) for _ in range(3))
    r_out, w_out, k_out, v_out, kk_out, b_out, g_out, shift_out = (next(it) for _ in range(8))
    last_scr = next(it)

    h = _rms(x_ref[...], gpre_ref[...])
    tm = h.shape[0]
    if single_step:
        prev = shift_ref[...]
        shift_out[...] = h
    else:
        i = pl.program_id(0)
        b = i // seq_tiles
        ti = i % seq_tiles
        first = jnp.where(ti == 0, shift_ref[pl.ds(b, 1), :], last_scr[...])
        prev = jnp.where(_iota((tm, 1), 0) == 0, first, pltpu.roll(h, 1, 0))
        last_scr[...] = h[tm - 1:tm, :]

        @pl.when(ti == seq_tiles - 1)
        def _():
            shift_out[pl.ds(b, 1), :] = h[tm - 1:tm, :]

    d = prev - h
    mix = lambda n: h + d * mu_ref[n:n + 1, :]
    w0, a0, k_k, k_a = (vec_ref[n:n + 1, :] for n in range(4))

    xv = mix(3)
    r = _bdot(mix(0), wr_ref[...])
    k = _bdot(mix(2), wk_ref[...])
    v = _bdot(xv, wv_ref[...])
    wl = w0 + _bdot(jnp.tanh(_bdot(mix(1), w1_ref[...])), w2_ref[...])
    a = _sigmoid(a0 + _bdot(_bdot(mix(4), a1_ref[...]), a2_ref[...]))
    if has_vres:
        gate = _sigmoid(vec_ref[4:5, :] + _bdot(_bdot(xv, v1_ref[...]), v2_ref[...]))
        v = v + (vfirst_ref[...] - v) * gate
    g = _bdot(_sigmoid(_bdot(mix(5), g1_ref[...])), g2_ref[...])

    kk = k * k_k
    kk = kk / jnp.maximum(jnp.sqrt(_seg_sum(kk * kk, RW_HEAD)), 1e-12)
    r_out[...] = r
    w_out[...] = -RW_DECAY_SCALE * _sigmoid(wl)
    k_out[...] = k * (1.0 + (a - 1.0) * k_a)
    v_out[...] = v
    kk_out[...] = kk
    b_out[...] = kk * a
    g_out[...] = g


def _rwkv_proj(x, shift0, gpre, mu, vecs, mats, vres, *, seq_len, tm):
    M, D = x.shape
    B = M // seq_len
    single_step = seq_len == 1
    seq_tiles = 1 if single_step else seq_len // tm
    row = pl.BlockSpec((tm, D), lambda i: (i, 0))
    shift_spec = row if single_step else _const_spec((B, D))
    shift_out_spec = row if single_step else pl.BlockSpec((B, D), lambda i: (0, 0))
    ins = [x, shift0, gpre, mu, vecs] + list(mats)
    specs = [row, shift_spec, _const_spec(gpre.shape), _const_spec(mu.shape), _const_spec(vecs.shape)]
    specs += [_const_spec(m.shape) for m in mats]
    if vres is not None:
        v1, v2, vfirst = vres
        ins += [v1, v2, vfirst]
        specs += [_const_spec(v1.shape), _const_spec(v2.shape), row]
    outs = [jax.ShapeDtypeStruct((M, D), F32)] * 7 + [jax.ShapeDtypeStruct((B, D), F32)]
    return pl.pallas_call(
        functools.partial(_rwkv_proj_kernel, seq_tiles=seq_tiles, has_vres=vres is not None,
                          single_step=single_step),
        grid=(M // tm,), in_specs=specs, out_specs=[row] * 7 + [shift_out_spec], out_shape=outs,
        scratch_shapes=[pltpu.VMEM((1, D), F32)],
        compiler_params=_params(("arbitrary",)), name="rwkv_proj")(*ins)


def _wkv_step_kernel(r_ref, lw_ref, k_ref, v_ref, kk_ref, b_ref, s0_ref, o_ref, sout_ref, *, pairs):
    n = RW_HEAD
    ones2 = _block_ones(LANES, n)
    diag2 = jnp.where(_iota((n, LANES), 0) == (_iota((n, LANES), 1) % n), 1.0, 0.0)
    for p in range(pairs):
        sl = slice(p * LANES, (p + 1) * LANES)
        s = jnp.concatenate([s0_ref[2 * p], s0_ref[2 * p + 1]], axis=1)
        lhs = jnp.concatenate([s * kk_ref[:, sl], diag2 * v_ref[:, sl]], axis=0)
        res = jnp.dot(lhs.astype(BF16), ones2, preferred_element_type=F32)
        s = s * jnp.exp(lw_ref[:, sl]) - res[:n] * b_ref[:, sl] + res[n:] * k_ref[:, sl]
        ob = jnp.dot((s * r_ref[:, sl]).astype(BF16), ones2, preferred_element_type=F32)
        o_ref[:, sl] = jnp.sum(ob * diag2, axis=0, keepdims=True)
        sout_ref[2 * p] = s[:, :n]
        sout_ref[2 * p + 1] = s[:, n:]


def _wkv_step(r, lw, k, v, kk, b, s0):
    B, _, D = r.shape
    H = s0.shape[1]
    seq = pl.BlockSpec((None, 1, D), lambda bi: (bi, 0, 0))
    st = pl.BlockSpec((None, H, RW_HEAD, RW_HEAD), lambda bi: (bi, 0, 0, 0))
    return pl.pallas_call(
        functools.partial(_wkv_step_kernel, pairs=H // 2), grid=(B,),
        in_specs=[seq] * 6 + [st], out_specs=[seq, st],
        out_shape=[jax.ShapeDtypeStruct((B, 1, D), F32), jax.ShapeDtypeStruct(s0.shape, F32)],
        compiler_params=_params(("arbitrary",)), name="wkv_step")(r, lw, k, v, kk, b, s0)


WKV_CHUNK = 64
WKV_GROUP = 256 // RW_HEAD


def _wkv_chunk_kernel(r_ref, lw_ref, k_ref, v_ref, kk_ref, b_ref, s0_ref, o_ref, sout_ref, s_scr, *, tblk, groups,
                      unroll):
    C, W = WKV_CHUNK, 256
    ti = pl.program_id(1)

    @pl.when(ti == 0)
    def _():
        s_scr[...] = s0_ref[...]

    blk = jnp.where((_iota((W, W), 0) // C) == (_iota((W, W), 1) // C), 1.0, 0.0)
    blkb = blk.astype(BF16)
    tcol, trow = _iota((C, W), 1) % C, _iota((C, W), 0)
    strict, incl = tcol < trow, tcol <= trow
    eye = jnp.where(tcol == trow, 1.0, 0.0)
    tri = jnp.where(_iota((C, C), 1) <= _iota((C, C), 0), 1.0, 0.0).astype(BF16)

    def bd(x):
        return jnp.concatenate([x.astype(BF16)] * (W // C), axis=0) * blkb

    def sbs(y):
        ym = y * blk
        return (ym[0:C] + ym[C:2 * C]) + (ym[2 * C:3 * C] + ym[3 * C:4 * C])

    mm = lambda a, b: jnp.dot(a.astype(BF16), b, preferred_element_type=F32)

    def chunk(c, carry):
        t0 = pl.multiple_of(c * (C * unroll), C * unroll)
        G = range(groups)
        units = [(u, g) for u in range(unroll) for g in G]
        each = lambda f, *cols: [f(*args) for args in zip(*cols)]
        load = lambda ref: [ref[pl.ds(t0 + u * C, C), g * W:(g + 1) * W] for u, g in units]
        r, lw, k, v, kk, b = (load(ref) for ref in (r_ref, lw_ref, k_ref, v_ref, kk_ref, b_ref))
        cum = each(lambda x: _sel_dot_left(tri, x), lw)
        tot = each(lambda x: x[C - 1:C, :], cum)
        e_neg = each(lambda x: jnp.exp(-x), cum)
        rt = each(lambda x, y: x * jnp.exp(y), r, cum)
        at = each(lambda x, y, z: x * jnp.exp(y - z), kk, cum, lw)
        lhs2 = each(lambda x, y: jnp.concatenate([x, y], axis=0).astype(BF16), at, rt)
        gb = each(lambda l, x, e: _dot_nt(l, bd(x * e)), lhs2, b, e_neg)
        gk = each(lambda l, x, e: _dot_nt(l, bd(x * e)), lhs2, k, e_neg)
        aab = each(lambda x: jnp.where(strict, x[:C], 0.0), gb)
        arb = each(lambda x: jnp.where(incl, x[C:], 0.0).astype(BF16), gb)
        a2k = each(lambda x: jnp.concatenate([jnp.where(strict, x[:C], 0.0), jnp.where(incl, x[C:], 0.0)], axis=0), gk)
        av = each(lambda x, y: mm(x, bd(y)), a2k, v)
        x = each(lambda n: eye - n, aab)
        pw = aab
        pbd = each(bd, pw)
        for _ in range(5):
            pw = each(mm, pw, pbd)
            pbd = each(bd, pw)
            x = each(lambda y, z: y + mm(y, z), x, pbd)
        xb = each(lambda y: y.astype(BF16), x)
        p = each(lambda y, z: mm(y, bd(z)), xb, at)
        q = each(lambda y, z: mm(y, bd(z[:C])), xb, av)
        rp = each(lambda y, a, z: y - mm(a, bd(z)), rt, arb, p)
        op = each(lambda y, a, z: y[C:] - mm(a, bd(z)), av, arb, q)
        e_end = each(lambda t, y: jnp.exp(t - y), tot, cum)
        kp = each(lambda y, e: y * e, k, e_end)
        bp = each(lambda y, e: y * e, b, e_end)
        m_s = each(lambda t, y, z: eye * jnp.exp(t) - sbs(_dot_tn(y, z)), tot, bp, p)
        n_s = each(lambda y, z, vv, qq: sbs(_dot_tn(jnp.concatenate([y, z], axis=0),
                                                    jnp.concatenate([vv, -qq], axis=0))), kp, bp, v, q)
        state = [s_scr[g] for g in G]
        for u in range(unroll):
            sbd = each(bd, state)
            for g in G:
                i = u * groups + g
                o_ref[pl.ds(t0 + u * C, C), g * W:(g + 1) * W] = mm(rp[i], sbd[g]) + op[i]
            state = [mm(m_s[u * groups + g], sbd[g]) + n_s[u * groups + g] for g in G]
        for g in G:
            s_scr[g] = state[g]
        return carry

    lax.fori_loop(0, tblk // (C * unroll), chunk, 0)

    @pl.when(ti == pl.num_programs(1) - 1)
    def _():
        sout_ref[...] = s_scr[...]


def _wkv_chunked(r, lw, k, v, kk, b, s0, *, tblk, unroll=2):
    B, T, D = r.shape
    H, N = s0.shape[1], s0.shape[2]
    G = WKV_GROUP
    pack = lambda s: jnp.transpose(s.reshape(B, H // G, G, N, N), (0, 1, 4, 2, 3)).reshape(B, H // G, N, G * N)
    unpack = lambda s: jnp.transpose(s.reshape(B, H // G, N, G, N), (0, 1, 3, 4, 2)).reshape(B, H, N, N)
    seq = pl.BlockSpec((None, tblk, D), lambda bi, ti: (bi, ti, 0))
    st = pl.BlockSpec((None, H // G, N, G * N), lambda bi, ti: (bi, 0, 0, 0))
    o, s_new = pl.pallas_call(
        functools.partial(_wkv_chunk_kernel, tblk=tblk, groups=H // G, unroll=unroll),
        grid=(B, T // tblk), in_specs=[seq] * 6 + [st], out_specs=[seq, st],
        out_shape=[jax.ShapeDtypeStruct((B, T, D), F32), jax.ShapeDtypeStruct((B, H // G, N, G * N), F32)],
        scratch_shapes=[pltpu.VMEM((H // G, N, G * N), F32)],
        compiler_params=_params(("arbitrary", "arbitrary")), name="wkv_chunked")(r, lw, k, v, kk, b, pack(s0))
    return o, unpack(s_new)


def _finish(y, wo_ref, x_ref, gpost_ref, out_ref):
    z = jnp.dot(y.astype(BF16), wo_ref[...], preferred_element_type=F32)
    out_ref[...] = x_ref[...] + _rms(z, gpost_ref[...])


def _rwkv_out_kernel(o_ref, r_ref, k_ref, v_ref, g_ref, x_ref, vec_ref, wo_ref, gpost_ref, out_ref):
    rk, lnw, lnb = (vec_ref[n:n + 1, :] for n in range(3))
    o = o_ref[...]
    dev = o - _seg_sum(o, RW_HEAD) * (1.0 / RW_HEAD)
    var = _seg_sum(dev * dev, RW_HEAD) * (1.0 / RW_HEAD)
    on = dev * lax.rsqrt(var + RW_GN_EPS) * lnw + lnb
    bonus = _seg_sum(r_ref[...] * k_ref[...] * rk, RW_HEAD) * v_ref[...]
    _finish((on + bonus) * g_ref[...], wo_ref, x_ref, gpost_ref, out_ref)


def _hgrn_out_kernel(o_ref, g_ref, x_ref, ng_ref, wo_ref, gpost_ref, out_ref):
    o = o_ref[...]
    parts = []
    for j in range(o.shape[1] // HG_HEAD):
        oj = o[:, j * HG_HEAD:(j + 1) * HG_HEAD]
        parts.append(oj * lax.rsqrt(jnp.mean(oj * oj, axis=-1, keepdims=True) + NORM_EPS))
    g = g_ref[...]
    y = jnp.concatenate(parts, axis=1) * ng_ref[...] * (g * _sigmoid(g))
    _finish(y, wo_ref, x_ref, gpost_ref, out_ref)


def _fox_out_kernel(o_ref, g_ref, x_ref, wo_ref, gpost_ref, out_ref):
    _finish(o_ref[...] * _sigmoid(g_ref[...]), wo_ref, x_ref, gpost_ref, out_ref)


def _mixer_out(body, rows, consts, *, tm, name):
    M, D = rows[0].shape
    row = pl.BlockSpec((tm, D), lambda i: (i, 0))
    return pl.pallas_call(
        body, grid=(M // tm,), in_specs=[row] * len(rows) + [_const_spec(c.shape) for c in consts],
        out_specs=row, out_shape=jax.ShapeDtypeStruct((M, D), F32),
        compiler_params=_params(("arbitrary",)), name=name)(*rows, *consts)


def _ffn_kernel(x_ref, gpre_ref, gpost_ref, wg_ref, wu_ref, wd_ref, out_ref):
    x = x_ref[...]
    h = _rms(x, gpre_ref[...]).astype(BF16)
    a = jnp.dot(h, wg_ref[...], preferred_element_type=F32)
    u = jnp.dot(h, wu_ref[...], preferred_element_type=F32)
    act = (a * _sigmoid(a) * u).astype(BF16)
    z = jnp.dot(act, wd_ref[...], preferred_element_type=F32)
    out_ref[...] = x + _rms(z, gpost_ref[...])


def _ffn(x, gpre, gpost, wg, wu, wd, *, tm):
    M, D = x.shape
    row = pl.BlockSpec((tm, D), lambda i: (i, 0))
    consts = [gpre, gpost, wg, wu, wd]
    return pl.pallas_call(
        _ffn_kernel, grid=(M // tm,), in_specs=[row] + [_const_spec(c.shape) for c in consts],
        out_specs=row, out_shape=jax.ShapeDtypeStruct((M, D), F32),
        compiler_params=_params(("arbitrary",)), name="ffn")(x, *consts)


def _hgrn_proj_kernel(x_ref, gpre_ref, lb_ref, w_ref, q_out, k_out, v_out, lf_out, g_out, *, layer):
    D = x_ref.shape[1]
    h = _rms(x_ref[...], gpre_ref[...]).astype(BF16)
    proj = lambda n: jnp.dot(h, w_ref[:, n * D:(n + 1) * D], preferred_element_type=F32)
    lbw = lb_ref[...]
    e = jnp.exp(lbw - jnp.max(lbw, axis=0, keepdims=True))
    lb = jnp.sum(e[1:layer + 1], axis=0, keepdims=True) / jnp.sum(e, axis=0, keepdims=True)
    q = proj(0)
    f = lb + (1.0 - lb) * _sigmoid(proj(1))
    q_out[...] = q * _sigmoid(q)
    k_out[...] = 1.0 - f
    lf_out[...] = jnp.log(f)
    v_out[...] = proj(2)
    g_out[...] = proj(3)


def _hgrn_proj(x, gpre, hg_lb, w_in, *, layer, tm):
    M, D = x.shape
    row = pl.BlockSpec((tm, D), lambda i: (i, 0))
    return pl.pallas_call(
        functools.partial(_hgrn_proj_kernel, layer=layer), grid=(M // tm,),
        in_specs=[row, _const_spec(gpre.shape), _const_spec(hg_lb.shape), _const_spec(w_in.shape)],
        out_specs=[row] * 5, out_shape=[jax.ShapeDtypeStruct((M, D), F32)] * 5,
        compiler_params=_params(("arbitrary",)), name="hgrn_proj")(x, gpre, hg_lb, w_in)


def _gla_kernel(q_ref, k_ref, v_ref, lf_ref, s0_ref, o_ref, sout_ref, s_scr, *, tblk, heads, unroll):
    ti = pl.program_id(1)
    C = HG_CHUNK

    @pl.when(ti == 0)
    def _():
        s_scr[...] = s0_ref[...]

    lower = _iota((C, C), 0) >= _iota((C, C), 1)
    tri = jnp.where(lower, 1.0, 0.0).astype(BF16)

    def chunk(c, carry):
        t0 = pl.multiple_of(c * (C * unroll), C * unroll)
        hs = range(heads)
        units = [(u, hd) for u in range(unroll) for hd in hs]
        each = lambda f, *cols: [f(*args) for args in zip(*cols)]
        load = lambda ref: [ref[pl.ds(t0 + u * C, C), hd * HG_HEAD:(hd + 1) * HG_HEAD] for u, hd in units]
        q, k, v, lf = load(q_ref), load(k_ref), load(v_ref), load(lf_ref)
        b = each(lambda x: _sel_dot_left(tri, x), lf)
        qd = each(lambda x, y: (x * jnp.exp(y)).astype(BF16), q, b)
        att = each(lambda x, y, z: jnp.where(lower, _dot_nt(x, y * jnp.exp(-z)), 0.0), qd, k, b)
        o2 = each(_bdot, att, v)
        kv = each(lambda x, y, z: _dot_tn(x, y * jnp.exp(z[C - 1:C, :] - z)), v, k, b)
        decay = each(lambda z: jnp.exp(z[C - 1:C, :]), b)
        st = [s_scr[hd] for hd in hs]
        for u in range(unroll):
            o1 = [_dot_nt(qd[u * heads + hd], st[hd]) for hd in hs]
            for hd in hs:
                i = u * heads + hd
                o_ref[pl.ds(t0 + u * C, C), hd * HG_HEAD:(hd + 1) * HG_HEAD] = o1[hd] + o2[i]
            st = [st[hd] * decay[u * heads + hd] + kv[u * heads + hd] for hd in hs]
        for hd in hs:
            s_scr[hd] = st[hd]
        return carry

    lax.fori_loop(0, tblk // (C * unroll), chunk, 0)

    @pl.when(ti == pl.num_programs(1) - 1)
    def _():
        sout_ref[...] = s_scr[...]


def _gla(q, k, v, lf, s0t, *, tblk, unroll=4):
    B, T, D = q.shape
    H = s0t.shape[1]
    seq = pl.BlockSpec((None, tblk, D), lambda bi, ti: (bi, ti, 0))
    st = pl.BlockSpec((None, H, HG_HEAD, HG_HEAD), lambda bi, ti: (bi, 0, 0, 0))
    return pl.pallas_call(
        functools.partial(_gla_kernel, tblk=tblk, heads=H, unroll=unroll),
        grid=(B, T // tblk), in_specs=[seq] * 4 + [st], out_specs=[seq, st],
        out_shape=[jax.ShapeDtypeStruct((B, T, D), F32), jax.ShapeDtypeStruct(s0t.shape, F32)],
        scratch_shapes=[pltpu.VMEM((H, HG_HEAD, HG_HEAD), F32)],
        compiler_params=_params(("arbitrary", "arbitrary")), name="gla_chunked")(q, k, v, lf, s0t)


def _gla_step_kernel(q_ref, k_ref, v_ref, lf_ref, s0_ref, o_ref, sout_ref, *, heads):
    n = HG_HEAD
    eye = jnp.where(_iota((n, n), 0) == _iota((n, n), 1), 1.0, 0.0)
    ones = jnp.ones((n, n), BF16)
    for hd in range(heads):
        sl = slice(hd * n, (hd + 1) * n)
        vcol = _sel_dot(eye * v_ref[:, sl], ones)
        st = s0_ref[hd] * jnp.exp(lf_ref[:, sl]) + vcol * k_ref[:, sl]
        sout_ref[hd] = st
        o_ref[:, sl] = _dot_nt(jnp.broadcast_to(q_ref[:, sl], (8, n)), st)[0:1, :]


def _gla_step(q, k, v, lf, s0t):
    B, _, D = q.shape
    H = s0t.shape[1]
    seq = pl.BlockSpec((None, 1, D), lambda bi: (bi, 0, 0))
    st = pl.BlockSpec((None, H, HG_HEAD, HG_HEAD), lambda bi: (bi, 0, 0, 0))
    return pl.pallas_call(
        functools.partial(_gla_step_kernel, heads=H), grid=(B,),
        in_specs=[seq] * 4 + [st], out_specs=[seq, st],
        out_shape=[jax.ShapeDtypeStruct((B, 1, D), F32), jax.ShapeDtypeStruct(s0t.shape, F32)],
        compiler_params=_params(("arbitrary",)), name="gla_step")(q, k, v, lf, s0t)


def _fox_proj_kernel(x_ref, gpre_ref, w_ref, wf_ref, bf_ref, qn_ref, kn_ref,
                     q_out, k_out, v_out, g_out, lf_out, c_out, carry_scr, *, seq_tiles, single_step):
    D = x_ref.shape[1]
    tm = x_ref.shape[0]
    h = _rms(x_ref[...], gpre_ref[...]).astype(BF16)
    proj = lambda n: jnp.dot(h, w_ref[:, n * D:(n + 1) * D], preferred_element_type=F32)
    head_norm = lambda t, gain: t * lax.rsqrt(_seg_sum(t * t, FX_HEAD) * (1.0 / FX_HEAD) + NORM_EPS) * gain
    q_out[...] = head_norm(proj(0), qn_ref[...]) * (FX_HEAD ** -0.5)
    k_out[...] = head_norm(proj(1), kn_ref[...])
    v_out[...] = proj(2)
    g_out[...] = proj(3)
    z = jnp.dot(h, wf_ref[...], preferred_element_type=F32) + bf_ref[...]
    lf = jnp.minimum(z, 0.0) - jnp.log1p(jnp.exp(-jnp.abs(z)))
    lf_out[...] = lf[:, :FX_HEADS]
    if single_step:
        c_out[...] = lf[:, :FX_HEADS]
    else:
        ti = pl.program_id(0) % seq_tiles

        @pl.when(ti == 0)
        def _():
            carry_scr[...] = jnp.zeros_like(carry_scr)

        tri = jnp.where(_iota((tm, tm), 0) >= _iota((tm, tm), 1), 1.0, 0.0).astype(BF16)
        c = _sel_dot_left(tri, lf) + carry_scr[...]
        carry_scr[...] = c[tm - 1:tm, :]
        c_out[...] = c[:, :FX_HEADS]


def _fox_proj(x, gpre, w_main, w_f, b_f, qn, kn, *, seq_len, tm):
    M, D = x.shape
    single_step = seq_len == 1
    row = pl.BlockSpec((tm, D), lambda i: (i, 0))
    hrow = pl.BlockSpec((tm, FX_HEADS), lambda i: (i, 0))
    consts = [gpre, w_main, w_f, b_f, qn, kn]
    return pl.pallas_call(
        functools.partial(_fox_proj_kernel, seq_tiles=1 if single_step else seq_len // tm,
                          single_step=single_step),
        grid=(M // tm,), in_specs=[row] + [_const_spec(c.shape) for c in consts],
        out_specs=[row] * 4 + [hrow] * 2,
        out_shape=[jax.ShapeDtypeStruct((M, D), F32)] * 4 + [jax.ShapeDtypeStruct((M, FX_HEADS), F32)] * 2,
        scratch_shapes=[pltpu.VMEM((1, LANES), F32)],
        compiler_params=_params(("arbitrary",)), name="fox_proj")(x, *consts)


def _fox_attn_kernel(q_ref, k_ref, vt_ref, cq_ref, ck_ref, o_ref, *, tq, tk, pairs):
    qi = pl.program_id(2)
    n = FX_HEAD
    heads = range(2 * pairs)
    first = _iota((1, LANES), 1) < n
    qh, cqh = [], []
    for p in range(pairs):
        q = q_ref[:, p * LANES:(p + 1) * LANES] * LOG2E
        qh += [jnp.where(first, q, 0.0).astype(BF16), jnp.where(first, 0.0, q).astype(BF16)]
        cq = cq_ref[p] * LOG2E
        cqh += [cq[0:1, :], cq[1:2, :]]
    full_blocks = (qi * tq) // tk

    def block(ki, carry, diagonal):
        k0 = pl.multiple_of(ki * tk, tk)
        kb = [k_ref[pl.ds(k0, tk), p * LANES:(p + 1) * LANES].astype(BF16) for p in range(pairs)]
        keep = None
        if diagonal:
            keep = (_iota((tk, tq), 0) - _iota((tk, tq), 1)) <= (qi * tq - k0)
        s = [_dot_nt(kb[h // 2], qh[h]) for h in heads]
        out = []
        probs = []
        for h in heads:
            m, l, _ = carry[h]
            ck = ck_ref[h // 2, pl.ds(k0, tk), :] * LOG2E
            sh = s[h] + cqh[h] - ck[:, h % 2:h % 2 + 1]
            if diagonal:
                sh = jnp.where(keep, sh, NEG_BIG)
            m_new = jnp.maximum(m, jnp.max(sh, axis=0, keepdims=True))
            p = jnp.exp2(sh - m_new)
            a = jnp.exp2(m - m_new)
            probs.append((m_new, l * a + jnp.sum(p, axis=0, keepdims=True), a, p.astype(BF16)))
        for h in heads:
            m_new, l_new, a, p = probs[h]
            vt = vt_ref[h // 2, ki, (h % 2) * n:(h % 2 + 1) * n, :].astype(BF16)
            out.append((m_new, l_new, carry[h][2] * a + jnp.dot(vt, p, preferred_element_type=F32)))
        return tuple(out)

    row = lambda val: jnp.full((1, tq), val, F32)
    init = tuple((row(NEG_BIG), row(0.0), jnp.zeros((n, tq), F32)) for _ in heads)
    carry = lax.fori_loop(0, full_blocks, lambda ki, c: block(ki, c, False), init)
    fin = block(full_blocks, carry, True)
    o_ref[...] = jnp.concatenate([acc / l for _, l, acc in fin], axis=0).T


def _fox_attn(q, k, v, c, *, tq, tk, pairs):
    B, T, D = q.shape
    hp = D // LANES
    W = pairs * LANES
    ck = jnp.transpose(c.reshape(B, T, hp, 2), (0, 2, 1, 3))
    cq = jnp.transpose(c.reshape(B, T, hp, 2), (0, 2, 3, 1))
    vt = jnp.transpose(v.reshape(B, T // tk, tk, hp, LANES), (0, 3, 1, 4, 2))
    qs = pl.BlockSpec((None, tq, W), lambda b, p, i: (b, i, p))
    ks = pl.BlockSpec((None, T, W), lambda b, p, i: (b, 0, p))
    vs = pl.BlockSpec((None, pairs, T // tk, LANES, tk), lambda b, p, i: (b, p, 0, 0, 0))
    cqs = pl.BlockSpec((None, pairs, 2, tq), lambda b, p, i: (b, p, 0, i))
    cks = pl.BlockSpec((None, pairs, T, 2), lambda b, p, i: (b, p, 0, 0))
    return pl.pallas_call(
        functools.partial(_fox_attn_kernel, tq=tq, tk=tk, pairs=pairs), grid=(B, hp // pairs, T // tq),
        in_specs=[qs, ks, vs, cqs, cks], out_specs=qs, out_shape=jax.ShapeDtypeStruct((B, T, D), F32),
        compiler_params=_params(("arbitrary",) * 3), name="fox_attn")(q, k, vt, cq, ck)


def _fox_paged_kernel(*refs, npg):
    it = iter(refs)
    _pt_ref = next(it)
    kt_refs, vt_refs, lf_refs = ([next(it) for _ in range(npg)] for _ in range(3))
    q_ref, kn_ref, vn_ref, cq_ref, o_ref, m_scr, l_scr, carry_scr, acc_scr = (next(it) for _ in range(9))
    g = pl.program_id(1)
    H, N, P = FX_HEADS, FX_HEAD, PAGE_SIZE
    D = H * N

    @pl.when(g == 0)
    def _():
        m_scr[...] = jnp.full_like(m_scr, NEG_BIG)
        l_scr[...] = jnp.zeros_like(l_scr)
        carry_scr[...] = jnp.zeros_like(carry_scr)
        acc_scr[...] = jnp.zeros_like(acc_scr)

    own = _iota((H, D), 0) == _iota((H, D), 1) // N
    qbd = jnp.where(own, q_ref[...], 0.0).astype(BF16)
    later = jnp.where(_iota((P, P), 0) > _iota((P, P), 1), 1.0, 0.0).astype(BF16)
    cq = cq_ref[...]
    carry = carry_scr[...]
    scores = [None] * npg
    for j in reversed(range(npg)):
        lf = lf_refs[j][...]
        s = jnp.dot(qbd, kt_refs[j][...].astype(BF16), preferred_element_type=F32)
        scores[j] = s + _sel_dot(lf, later) + (carry + cq)
        carry = carry + jnp.sum(lf, axis=1, keepdims=True)
    carry_scr[...] = carry
    m_old = m_scr[...]
    m_new = m_old
    for j in range(npg):
        m_new = jnp.maximum(m_new, jnp.max(scores[j], axis=1, keepdims=True))
    alpha = jnp.exp(m_old - m_new)
    l_new = l_scr[...] * alpha
    acc = acc_scr[...] * alpha
    for j in range(npg):
        p = jnp.exp(scores[j] - m_new)
        l_new = l_new + jnp.sum(p, axis=1, keepdims=True)
        acc = acc + _dot_nt(p, vt_refs[j][...])
    acc_scr[...] = acc
    m_scr[...] = m_new
    l_scr[...] = l_new

    @pl.when(g == pl.num_programs(1) - 1)
    def _():
        s_new = jnp.sum(jnp.where(own, q_ref[...] * kn_ref[...], 0.0), axis=1, keepdims=True)
        m_fin = jnp.maximum(m_new, s_new)
        a_fin = jnp.exp(m_new - m_fin)
        p_new = jnp.exp(s_new - m_fin)
        mix = (acc * a_fin + vn_ref[...] * p_new) / (l_new * a_fin + p_new)
        o_ref[...] = jnp.sum(jnp.where(own, mix, 0.0), axis=0, keepdims=True)


def _fox_paged(q, k_new, v_new, lf_new, cache_kt, cache_vt, cache_lft, page_table, *, npg):
    B, _, D = q.shape
    H, P = FX_HEADS, PAGE_SIZE
    n_pages = page_table.shape[1]
    groups = n_pages // npg

    def page(j):
        return lambda b, g, pt: (pt[b, (groups - 1 - g) * npg + j], 0, 0)

    kspecs = [pl.BlockSpec((None, D, P), page(j)) for j in range(npg)]
    lspecs = [pl.BlockSpec((None, H, P), page(j)) for j in range(npg)]
    tok = pl.BlockSpec((None, 1, D), lambda b, g, pt: (b, 0, 0))
    tokh = pl.BlockSpec((None, H, 1), lambda b, g, pt: (b, 0, 0))
    grid_spec = pltpu.PrefetchScalarGridSpec(
        num_scalar_prefetch=1, grid=(B, groups),
        in_specs=kspecs + kspecs + lspecs + [tok, tok, tok, tokh], out_specs=tok,
        scratch_shapes=[pltpu.VMEM((H, 1), F32), pltpu.VMEM((H, 1), F32), pltpu.VMEM((H, 1), F32),
                        pltpu.VMEM((H, D), F32)])
    return pl.pallas_call(
        functools.partial(_fox_paged_kernel, npg=npg), grid_spec=grid_spec,
        out_shape=jax.ShapeDtypeStruct((B, 1, D), F32),
        compiler_params=_params(("arbitrary", "arbitrary")), name="fox_paged")(
            page_table, *([cache_kt] * npg), *([cache_vt] * npg), *([cache_lft] * npg),
            q, k_new, v_new, lf_new.reshape(B, H, 1))


def _row_tile(m, want):
    return want if m % want == 0 else m


def kernel(x_prompt, x_sample, state_rwkv_wkv, state_rwkv_shift, state_hgrn, cache_fox_k, cache_fox_v, cache_fox_logf, page_table, norm_pre_mix, norm_post_mix, norm_pre_ffn, norm_post_ffn, rw_mu, rw_wr, rw_wk, rw_wv, rw_wo, rw_w0, rw_w1, rw_w2, rw_a0, rw_a1, rw_a2, rw_v0, rw_v1, rw_v2, rw_g1, rw_g2, rw_kk, rw_ka, rw_rk, rw_lnw, rw_lnb, hg_w_in, hg_lb, hg_norm, hg_wo, fx_w_in, fx_bf, fx_qn, fx_kn, fx_wo, ffn_wg, ffn_wu, ffn_wd):
    D = D_MODEL
    bf = lambda t: t.astype(BF16)
    vec = lambda t: t.reshape(1, -1).astype(F32)

    n_cache_pages = cache_fox_k.shape[1]
    cache_k = jnp.transpose(cache_fox_k, (0, 1, 3, 4, 2)).reshape(-1, D, PAGE_SIZE)
    cache_v = jnp.transpose(cache_fox_v, (0, 1, 3, 4, 2)).reshape(-1, D, PAGE_SIZE)
    cache_lf = jnp.transpose(cache_fox_logf, (0, 1, 3, 2)).reshape(-1, FX_HEADS, PAGE_SIZE)

    def trunk(x3, wkv0, shift0, hg0, paged):
        B, T, _ = x3.shape
        M = B * T
        x = x3.reshape(M, D)
        tm = _row_tile(M, 256)
        v_first = None
        wkv_n, shift_n, hg_n, k_n, v_n, lf_n = [], [], [], [], [], []
        for i in range(DEPTH):
            j = i // N_MIXERS
            gpre, gpost = vec(norm_pre_mix[i]), vec(norm_post_mix[i])
            if i % N_MIXERS == 0:
                vecs = [rw_w0[j], rw_a0[j], rw_kk[j], rw_ka[j]]
                vecs += [rw_v0[j - 1]] if j > 0 else []
                vecs = jnp.stack(vecs + [jnp.zeros((D,), F32)] * (8 - len(vecs)))
                mats = [bf(t[j]) for t in (rw_wr, rw_wk, rw_wv, rw_w1, rw_w2, rw_a1, rw_a2, rw_g1, rw_g2)]
                vres = None if j == 0 else (bf(rw_v1[j - 1]), bf(rw_v2[j - 1]), v_first)
                r, w, k, v, kk, kb, g, shift = _rwkv_proj(x, shift0[j], gpre, rw_mu[j], vecs, mats, vres,
                                                          seq_len=T, tm=tm)
                if j == 0:
                    v_first = v
                seq = lambda t: t.reshape(B, T, D)
                wkv = _wkv_step if T == 1 else functools.partial(_wkv_chunked, tblk=min(T, 256))
                o, s_new = wkv(seq(r), seq(w), seq(k), seq(v), seq(kk), seq(kb), wkv0[j])
                ovec = jnp.stack([rw_rk[j].reshape(-1), rw_lnw[j], rw_lnb[j]] + [jnp.zeros((D,), F32)] * 5)
                x = _mixer_out(_rwkv_out_kernel, [o.reshape(M, D), r, k, v, g, x],
                               [ovec, bf(rw_wo[j]), gpost], tm=tm, name="rwkv_out")
                wkv_n.append(s_new)
                shift_n.append(shift)
            elif i % N_MIXERS == 1:
                q, k, v, lf, g = _hgrn_proj(x, gpre, hg_lb, bf(hg_w_in[j]), layer=i, tm=tm)
                seq = lambda t: t.reshape(B, T, D)
                s0t = jnp.swapaxes(hg0[j], -1, -2)
                if T == 1:
                    o, st = _gla_step(seq(q), seq(k), seq(v), seq(lf), s0t)
                else:
                    o, st = _gla(seq(q), seq(k), seq(v), seq(lf), s0t, tblk=min(T, 256))
                x = _mixer_out(_hgrn_out_kernel, [o.reshape(M, D), g, x],
                               [vec(hg_norm[j]), bf(hg_wo[j]), gpost], tm=tm, name="hgrn_out")
                hg_n.append(jnp.swapaxes(st, -1, -2))
            else:
                w_in = fx_w_in[j]
                w_f = jnp.pad(w_in[:, 4 * D:], ((0, 0), (0, LANES - FX_HEADS)))
                b_f = jnp.pad(fx_bf[j], (0, LANES - FX_HEADS)).reshape(1, LANES)
                tile = lambda t: jnp.tile(t, FX_HEADS).reshape(1, D)
                q, k, v, g, lf, c = _fox_proj(x, gpre, bf(w_in[:, :4 * D]), bf(w_f), b_f,
                                              tile(fx_qn[j]), tile(fx_kn[j]), seq_len=T, tm=tm)
                seq = lambda t: t.reshape(B, T, D)
                if paged:
                    pt = page_table + j * n_cache_pages
                    att = _fox_paged(seq(q), seq(k), seq(v), lf.reshape(B, 1, FX_HEADS),
                                     cache_k, cache_v, cache_lf, pt, npg=8)
                else:
                    att = _fox_attn(seq(q), seq(k), seq(v), c.reshape(B, T, FX_HEADS), tq=256, tk=512, pairs=2)
                x = _mixer_out(_fox_out_kernel, [att.reshape(M, D), g, x], [bf(fx_wo[j]), gpost],
                               tm=tm, name="fox_out")
                k_n.append(k.reshape(B, T, FX_HEADS, FX_HEAD))
                v_n.append(v.reshape(B, T, FX_HEADS, FX_HEAD))
                lf_n.append(lf.reshape(B, T, FX_HEADS))
            x = _ffn(x, vec(norm_pre_ffn[i]), vec(norm_post_ffn[i]), bf(ffn_wg[i]), bf(ffn_wu[i]),
                     bf(ffn_wd[i]), tm=tm)
        return (x.reshape(B, T, D), jnp.stack(wkv_n), jnp.stack(shift_n), jnp.stack(hg_n),
                jnp.stack(k_n), jnp.stack(v_n), jnp.stack(lf_n))

    bp = x_prompt.shape[0]
    n_a, n_b = state_rwkv_wkv.shape[0], state_hgrn.shape[0]
    y_p, wkv_p, shift_p, hg_p, k_p, v_p, lf_p = trunk(
        x_prompt, jnp.zeros((n_a, bp) + state_rwkv_wkv.shape[2:], F32), jnp.zeros((n_a, bp, D), F32),
        jnp.zeros((n_b, bp) + state_hgrn.shape[2:], F32), False)
    y_s, wkv_s, shift_s, hg_s, k_s, v_s, lf_s = trunk(
        x_sample, state_rwkv_wkv, state_rwkv_shift, state_hgrn, True)
    return (y_p, y_s, wkv_p, wkv_s, shift_p, shift_s, hg_p, hg_s, k_p, k_s, v_p, v_s, lf_p, lf_s)
```

```python
import functools
import math

import jax
import jax.numpy as jnp
from jax import lax
from jax.experimental import pallas as pl
from jax.experimental.pallas import tpu as pltpu

F32 = jnp.float32
BF16 = jnp.bfloat16

LANES = 128
VMEM_LIMIT = 56 * 1024 * 1024

D_MODEL = 1024
DEPTH = 4
N_MIXERS = 3
RW_HEAD = 64
RW_HEADS = D_MODEL // RW_HEAD
RW_DECAY_SCALE = 0.6065306597126334
RW_GN_EPS = 64e-5
HG_HEADS = 8
HG_HEAD = D_MODEL // HG_HEADS
HG_CHUNK = 16
FX_HEAD = 64
FX_HEADS = D_MODEL // FX_HEAD
PAGE_SIZE = 128
D_FF = 2816
NORM_EPS = 1e-6
NEG_BIG = -1e30
LOG2E = 1.4426950408889634


def _params(sem):
    return pltpu.CompilerParams(dimension_semantics=sem, vmem_limit_bytes=VMEM_LIMIT)


def _const_spec(shape):
    n = len(shape)
    return pl.BlockSpec(shape, lambda *_: (0,) * n, pipeline_mode=pl.Buffered(1))


def _rms(x, g, eps=NORM_EPS):
    return x * lax.rsqrt(jnp.mean(x * x, axis=-1, keepdims=True) + eps) * g


def _sigmoid(x):
    return 1.0 / (1.0 + jnp.exp(-x))


def _bdot(a, b):
    return jnp.dot(a.astype(BF16), b.astype(BF16), preferred_element_type=F32)


def _dot_nt(a, b):
    return lax.dot_general(a.astype(BF16), b.astype(BF16), (((1,), (1,)), ((), ())),
                           preferred_element_type=F32)


def _dot_tn(a, b):
    return lax.dot_general(a.astype(BF16), b.astype(BF16), (((0,), (0,)), ((), ())),
                           preferred_element_type=F32)


def _split(x):
    hi = x.astype(BF16)
    lo = (x - hi.astype(F32)).astype(BF16)
    return hi, lo


def _sel_dot(x, sel):
    hi, lo = _split(x)
    return (jnp.dot(hi, sel, preferred_element_type=F32)
            + jnp.dot(lo, sel, preferred_element_type=F32))


def _sel_dot_left(sel, x):
    hi, lo = _split(x)
    return (jnp.dot(sel, hi, preferred_element_type=F32)
            + jnp.dot(sel, lo, preferred_element_type=F32))


def _iota(shape, dim):
    return lax.broadcasted_iota(jnp.int32, shape, dim)


def _block_ones(n, seg):
    same = (_iota((n, n), 0) // seg) == (_iota((n, n), 1) // seg)
    return jnp.where(same, 1.0, 0.0).astype(BF16)


def _seg_sum(x, seg):
    sel = _block_ones(LANES, seg)
    parts = [_sel_dot(x[:, j * LANES:(j + 1) * LANES], sel) for j in range(x.shape[1] // LANES)]
    return jnp.concatenate(parts, axis=1)


def _rwkv_proj_kernel(*refs, seq_tiles, has_vres, single_step):
    it = iter(refs)
    x_ref, shift_ref, gpre_ref, mu_ref, vec_ref = (next(it) for _ in range(5))
    wr_ref, wk_ref, wv_ref, w1_ref, w2_ref, a1_ref, a2_ref, g1_ref, g2_ref = (next(it) for _ in range(9))
    if has_vres:
        v1_ref, v2_ref, vfirst_ref = (next(it) for _ in range(3))
    r_out, w_out, k_out, v_out, kk_out, b_out, g_out, shift_out = (next(it) for _ in range(8))
    last_scr = next(it)

    h = _rms(x_ref[...], gpre_ref[...])
    tm = h.shape[0]
    if single_step:
        prev = shift_ref[...]
        shift_out[...] = h
    else:
        i = pl.program_id(0)
        b = i // seq_tiles
        ti = i % seq_tiles
        first = jnp.where(ti == 0, shift_ref[pl.ds(b, 1), :], last_scr[...])
        prev = jnp.where(_iota((tm, 1), 0) == 0, first, pltpu.roll(h, 1, 0))
        last_scr[...] = h[tm - 1:tm, :]

        @pl.when(ti == seq_tiles - 1)
        def _():
            shift_out[pl.ds(b, 1), :] = h[tm - 1:tm, :]

    d = prev - h
    mix = lambda n: h + d * mu_ref[n:n + 1, :]
    w0, a0, k_k, k_a = (vec_ref[n:n + 1, :] for n in range(4))

    xv = mix(3)
    r = _bdot(mix(0), wr_ref[...])
    k = _bdot(mix(2), wk_ref[...])
    v = _bdot(xv, wv_ref[...])
    wl = w0 + _bdot(jnp.tanh(_bdot(mix(1), w1_ref[...])), w2_ref[...])
    a = _sigmoid(a0 + _bdot(_bdot(mix(4), a1_ref[...]), a2_ref[...]))
    if has_vres:
        gate = _sigmoid(vec_ref[4:5, :] + _bdot(_bdot(xv, v1_ref[...]), v2_ref[...]))
        v = v + (vfirst_ref[...].astype(F32) - v) * gate
    g = _bdot(_sigmoid(_bdot(mix(5), g1_ref[...])), g2_ref[...])

    kk = k * k_k
    kk = kk / jnp.maximum(jnp.sqrt(_seg_sum(kk * kk, RW_HEAD)), 1e-12)
    put = lambda ref, val: ref.__setitem__(Ellipsis, val.astype(ref.dtype))
    put(r_out, r)
    w_out[...] = -RW_DECAY_SCALE * _sigmoid(wl)
    put(k_out, k * (1.0 + (a - 1.0) * k_a))
    put(v_out, v)
    put(kk_out, kk)
    put(b_out, kk * a)
    put(g_out, g)


def _rwkv_proj(x, shift0, gpre, mu, vecs, mats, vres, *, seq_len, tm, act_dtype=F32):
    M, D = x.shape
    B = M // seq_len
    single_step = seq_len == 1
    seq_tiles = 1 if single_step else seq_len // tm
    row = pl.BlockSpec((tm, D), lambda i: (i, 0))
    shift_spec = row if single_step else _const_spec((B, D))
    shift_out_spec = row if single_step else pl.BlockSpec((B, D), lambda i: (0, 0))
    ins = [x, shift0, gpre, mu, vecs] + list(mats)
    specs = [row, shift_spec, _const_spec(gpre.shape), _const_spec(mu.shape), _const_spec(vecs.shape)]
    specs += [_const_spec(m.shape) for m in mats]
    if vres is not None:
        v1, v2, vfirst = vres
        ins += [v1, v2, vfirst]
        specs += [_const_spec(v1.shape), _const_spec(v2.shape), row]
    act = jax.ShapeDtypeStruct((M, D), act_dtype)
    outs = [act, jax.ShapeDtypeStruct((M, D), F32)] + [act] * 5 + [jax.ShapeDtypeStruct((B, D), F32)]
    return pl.pallas_call(
        functools.partial(_rwkv_proj_kernel, seq_tiles=seq_tiles, has_vres=vres is not None,
                          single_step=single_step),
        grid=(M // tm,), in_specs=specs, out_specs=[row] * 7 + [shift_out_spec], out_shape=outs,
        scratch_shapes=[pltpu.VMEM((1, D), F32)],
        compiler_params=_params(("arbitrary",)), name="rwkv_proj")(*ins)


def _wkv_step_kernel(r_ref, lw_ref, k_ref, v_ref, kk_ref, b_ref, s0_ref, o_ref, sout_ref, *, pairs):
    n = RW_HEAD
    ones2 = _block_ones(LANES, n)
    diag2 = jnp.where(_iota((n, LANES), 0) == (_iota((n, LANES), 1) % n), 1.0, 0.0)
    for p in range(pairs):
        sl = slice(p * LANES, (p + 1) * LANES)
        s = jnp.concatenate([s0_ref[2 * p], s0_ref[2 * p + 1]], axis=1)
        lhs = jnp.concatenate([s * kk_ref[:, sl], diag2 * v_ref[:, sl]], axis=0)
        res = jnp.dot(lhs.astype(BF16), ones2, preferred_element_type=F32)
        s = s * jnp.exp(lw_ref[:, sl]) - res[:n] * b_ref[:, sl] + res[n:] * k_ref[:, sl]
        ob = jnp.dot((s * r_ref[:, sl]).astype(BF16), ones2, preferred_element_type=F32)
        o_ref[:, sl] = jnp.sum(ob * diag2, axis=0, keepdims=True)
        sout_ref[2 * p] = s[:, :n]
        sout_ref[2 * p + 1] = s[:, n:]


def _wkv_step(r, lw, k, v, kk, b, s0):
    B, _, D = r.shape
    H = s0.shape[1]
    seq = pl.BlockSpec((None, 1, D), lambda bi: (bi, 0, 0))
    st = pl.BlockSpec((None, H, RW_HEAD, RW_HEAD), lambda bi: (bi, 0, 0, 0))
    return pl.pallas_call(
        functools.partial(_wkv_step_kernel, pairs=H // 2), grid=(B,),
        in_specs=[seq] * 6 + [st], out_specs=[seq, st],
        out_shape=[jax.ShapeDtypeStruct((B, 1, D), F32), jax.ShapeDtypeStruct(s0.shape, F32)],
        compiler_params=_params(("arbitrary",)), name="wkv_step")(r, lw, k, v, kk, b, s0)


WKV_CHUNK = 64
WKV_GROUP = 256 // RW_HEAD


def _wkv_chunk_kernel(r_ref, lw_ref, k_ref, v_ref, kk_ref, b_ref, s0_ref, o_ref, sout_ref, s_scr, *, tblk, groups,
                      unroll):
    C, W = WKV_CHUNK, 256
    ti = pl.program_id(1)

    @pl.when(ti == 0)
    def _():
        s_scr[...] = s0_ref[...]

    blk = jnp.where((_iota((W, W), 0) // C) == (_iota((W, W), 1) // C), 1.0, 0.0)
    blkb = blk.astype(BF16)
    tcol, trow = _iota((C, W), 1) % C, _iota((C, W), 0)
    strict, incl = tcol < trow, tcol <= trow
    eye = jnp.where(tcol == trow, 1.0, 0.0)
    tri = jnp.where(_iota((C, C), 1) <= _iota((C, C), 0), 1.0, 0.0).astype(BF16)

    def bd(x):
        return jnp.concatenate([x.astype(BF16)] * (W // C), axis=0) * blkb

    def sbs(y):
        ym = y * blk
        return (ym[0:C] + ym[C:2 * C]) + (ym[2 * C:3 * C] + ym[3 * C:4 * C])

    mm = lambda a, b: jnp.dot(a.astype(BF16), b, preferred_element_type=F32)

    def chunk(c, carry):
        t0 = pl.multiple_of(c * (C * unroll), C * unroll)
        G = range(groups)
        units = [(u, g) for u in range(unroll) for g in G]
        each = lambda f, *cols: [f(*args) for args in zip(*cols)]
        load = lambda ref: [ref[pl.ds(t0 + u * C, C), g * W:(g + 1) * W].astype(F32) for u, g in units]
        r, lw, k, v, kk, b = (load(ref) for ref in (r_ref, lw_ref, k_ref, v_ref, kk_ref, b_ref))
        cum = each(lambda x: _sel_dot_left(tri, x), lw)
        tot = each(lambda x: x[C - 1:C, :], cum)
        e_neg = each(lambda x: jnp.exp(-x), cum)
        rt = each(lambda x, y: x * jnp.exp(y), r, cum)
        at = each(lambda x, y, z: x * jnp.exp(y - z), kk, cum, lw)
        lhs2 = each(lambda x, y: jnp.concatenate([x, y], axis=0).astype(BF16), at, rt)
        gb = each(lambda l, x, e: _dot_nt(l, bd(x * e)), lhs2, b, e_neg)
        gk = each(lambda l, x, e: _dot_nt(l, bd(x * e)), lhs2, k, e_neg)
        aab = each(lambda x: jnp.where(strict, x[:C], 0.0), gb)
        arb = each(lambda x: jnp.where(incl, x[C:], 0.0).astype(BF16), gb)
        a2k = each(lambda x: jnp.concatenate([jnp.where(strict, x[:C], 0.0), jnp.where(incl, x[C:], 0.0)], axis=0), gk)
        av = each(lambda x, y: mm(x, bd(y)), a2k, v)
        x = each(lambda n: eye - n, aab)
        pw = aab
        pbd = each(bd, pw)
        for _ in range(5):
            pw = each(mm, pw, pbd)
            pbd = each(bd, pw)
            x = each(lambda y, z: y + mm(y, z), x, pbd)
        xb = each(lambda y: y.astype(BF16), x)
        p = each(lambda y, z: mm(y, bd(z)), xb, at)
        q = each(lambda y, z: mm(y, bd(z[:C])), xb, av)
        rp = each(lambda y, a, z: y - mm(a, bd(z)), rt, arb, p)
        op = each(lambda y, a, z: y[C:] - mm(a, bd(z)), av, arb, q)
        e_end = each(lambda t, y: jnp.exp(t - y), tot, cum)
        kp = each(lambda y, e: y * e, k, e_end)
        bp = each(lambda y, e: y * e, b, e_end)
        m_s = each(lambda t, y, z: eye * jnp.exp(t) - sbs(_dot_tn(y, z)), tot, bp, p)
        n_s = each(lambda y, z, vv, qq: sbs(_dot_tn(jnp.concatenate([y, z], axis=0),
                                                    jnp.concatenate([vv, -qq], axis=0))), kp, bp, v, q)
        state = [s_scr[g] for g in G]
        for u in range(unroll):
            sbd = each(bd, state)
            for g in G:
                i = u * groups + g
                o_ref[pl.ds(t0 + u * C, C), g * W:(g + 1) * W] = mm(rp[i], sbd[g]) + op[i]
            state = [mm(m_s[u * groups + g], sbd[g]) + n_s[u * groups + g] for g in G]
        for g in G:
            s_scr[g] = state[g]
        return carry

    lax.fori_loop(0, tblk // (C * unroll), chunk, 0)

    @pl.when(ti == pl.num_programs(1) - 1)
    def _():
        sout_ref[...] = s_scr[...]


def _wkv_chunked(r, lw, k, v, kk, b, s0, *, tblk, unroll=2):
    B, T, D = r.shape
    H, N = s0.shape[1], s0.shape[2]
    G = WKV_GROUP
    pack = lambda s: jnp.transpose(s.reshape(B, H // G, G, N, N), (0, 1, 4, 2, 3)).reshape(B, H // G, N, G * N)
    unpack = lambda s: jnp.transpose(s.reshape(B, H // G, N, G, N), (0, 1, 3, 4, 2)).reshape(B, H, N, N)
    seq = pl.BlockSpec((None, tblk, D), lambda bi, ti: (bi, ti, 0))
    st = pl.BlockSpec((None, H // G, N, G * N), lambda bi, ti: (bi, 0, 0, 0))
    o, s_new = pl.pallas_call(
        functools.partial(_wkv_chunk_kernel, tblk=tblk, groups=H // G, unroll=unroll),
        grid=(B, T // tblk), in_specs=[seq] * 6 + [st], out_specs=[seq, st],
        out_shape=[jax.ShapeDtypeStruct((B, T, D), F32), jax.ShapeDtypeStruct((B, H // G, N, G * N), F32)],
        scratch_shapes=[pltpu.VMEM((H // G, N, G * N), F32)],
        compiler_params=_params(("arbitrary", "arbitrary")), name="wkv_chunked")(r, lw, k, v, kk, b, pack(s0))
    return o, unpack(s_new)


def _finish(y, wo_ref, x_ref, gpost_ref, out_ref):
    z = jnp.dot(y.astype(BF16), wo_ref[...], preferred_element_type=F32)
    out_ref[...] = x_ref[...] + _rms(z, gpost_ref[...])


def _rwkv_out_kernel(o_ref, r_ref, k_ref, v_ref, g_ref, x_ref, vec_ref, wo_ref, gpost_ref, out_ref):
    rk, lnw, lnb = (vec_ref[n:n + 1, :] for n in range(3))
    o = o_ref[...]
    dev = o - _seg_sum(o, RW_HEAD) * (1.0 / RW_HEAD)
    var = _seg_sum(dev * dev, RW_HEAD) * (1.0 / RW_HEAD)
    on = dev * lax.rsqrt(var + RW_GN_EPS) * lnw + lnb
    f32 = lambda ref: ref[...].astype(F32)
    bonus = _seg_sum(f32(r_ref) * f32(k_ref) * rk, RW_HEAD) * f32(v_ref)
    _finish((on + bonus) * f32(g_ref), wo_ref, x_ref, gpost_ref, out_ref)


def _hgrn_out_kernel(o_ref, g_ref, x_ref, ng_ref, wo_ref, gpost_ref, out_ref):
    o = o_ref[...]
    parts = []
    for j in range(o.shape[1] // HG_HEAD):
        oj = o[:, j * HG_HEAD:(j + 1) * HG_HEAD]
        parts.append(oj * lax.rsqrt(jnp.mean(oj * oj, axis=-1, keepdims=True) + NORM_EPS))
    g = g_ref[...].astype(F32)
    y = jnp.concatenate(parts, axis=1) * ng_ref[...] * (g * _sigmoid(g))
    _finish(y, wo_ref, x_ref, gpost_ref, out_ref)


def _fox_out_kernel(o_ref, g_ref, x_ref, wo_ref, gpost_ref, out_ref):
    _finish(o_ref[...] * _sigmoid(g_ref[...].astype(F32)), wo_ref, x_ref, gpost_ref, out_ref)


def _mixer_out(body, rows, consts, *, tm, name):
    M, D = rows[0].shape
    row = pl.BlockSpec((tm, D), lambda i: (i, 0))
    return pl.pallas_call(
        body, grid=(M // tm,), in_specs=[row] * len(rows) + [_const_spec(c.shape) for c in consts],
        out_specs=row, out_shape=jax.ShapeDtypeStruct((M, D), F32),
        compiler_params=_params(("arbitrary",)), name=name)(*rows, *consts)


def _ffn_kernel(x_ref, gpre_ref, gpost_ref, wg_ref, wu_ref, wd_ref, out_ref):
    x = x_ref[...]
    h = _rms(x, gpre_ref[...]).astype(BF16)
    a = jnp.dot(h, wg_ref[...], preferred_element_type=F32)
    u = jnp.dot(h, wu_ref[...], preferred_element_type=F32)
    act = (a * _sigmoid(a) * u).astype(BF16)
    z = jnp.dot(act, wd_ref[...], preferred_element_type=F32)
    out_ref[...] = x + _rms(z, gpost_ref[...])


def _ffn(x, gpre, gpost, wg, wu, wd, *, tm):
    M, D = x.shape
    row = pl.BlockSpec((tm, D), lambda i: (i, 0))
    consts = [gpre, gpost, wg, wu, wd]
    return pl.pallas_call(
        _ffn_kernel, grid=(M // tm,), in_specs=[row] + [_const_spec(c.shape) for c in consts],
        out_specs=row, out_shape=jax.ShapeDtypeStruct((M, D), F32),
        compiler_params=_params(("arbitrary",)), name="ffn")(x, *consts)


def _hgrn_proj_kernel(x_ref, gpre_ref, lb_ref, w_ref, q_out, k_out, v_out, lf_out, g_out, *, layer):
    D = x_ref.shape[1]
    h = _rms(x_ref[...], gpre_ref[...]).astype(BF16)
    proj = lambda n: jnp.dot(h, w_ref[:, n * D:(n + 1) * D], preferred_element_type=F32)
    lbw = lb_ref[...]
    e = jnp.exp(lbw - jnp.max(lbw, axis=0, keepdims=True))
    lb = jnp.sum(e[1:layer + 1], axis=0, keepdims=True) / jnp.sum(e, axis=0, keepdims=True)
    q = proj(0)
    f = lb + (1.0 - lb) * _sigmoid(proj(1))
    q_out[...] = (q * _sigmoid(q)).astype(q_out.dtype)
    k_out[...] = (1.0 - f).astype(k_out.dtype)
    lf_out[...] = jnp.log(f)
    v_out[...] = proj(2).astype(v_out.dtype)
    g_out[...] = proj(3).astype(g_out.dtype)


def _hgrn_proj(x, gpre, hg_lb, w_in, *, layer, tm, act_dtype=F32):
    M, D = x.shape
    row = pl.BlockSpec((tm, D), lambda i: (i, 0))
    return pl.pallas_call(
        functools.partial(_hgrn_proj_kernel, layer=layer), grid=(M // tm,),
        in_specs=[row, _const_spec(gpre.shape), _const_spec(hg_lb.shape), _const_spec(w_in.shape)],
        out_specs=[row] * 5,
        out_shape=[jax.ShapeDtypeStruct((M, D), dt) for dt in (act_dtype, act_dtype, act_dtype, F32, act_dtype)],
        compiler_params=_params(("arbitrary",)), name="hgrn_proj")(x, gpre, hg_lb, w_in)


def _gla_kernel(q_ref, k_ref, v_ref, lf_ref, s0_ref, o_ref, sout_ref, s_scr, *, tblk, heads, unroll):
    ti = pl.program_id(1)
    C = HG_CHUNK

    @pl.when(ti == 0)
    def _():
        s_scr[...] = s0_ref[...]

    lower = _iota((C, C), 0) >= _iota((C, C), 1)
    tri = jnp.where(lower, 1.0, 0.0).astype(BF16)

    def chunk(c, carry):
        t0 = pl.multiple_of(c * (C * unroll), C * unroll)
        hs = range(heads)
        units = [(u, hd) for u in range(unroll) for hd in hs]
        each = lambda f, *cols: [f(*args) for args in zip(*cols)]
        load = lambda ref: [ref[pl.ds(t0 + u * C, C), hd * HG_HEAD:(hd + 1) * HG_HEAD].astype(F32)
                            for u, hd in units]
        q, k, v, lf = load(q_ref), load(k_ref), load(v_ref), load(lf_ref)
        b = each(lambda x: _sel_dot_left(tri, x), lf)
        qd = each(lambda x, y: (x * jnp.exp(y)).astype(BF16), q, b)
        att = each(lambda x, y, z: jnp.where(lower, _dot_nt(x, y * jnp.exp(-z)), 0.0), qd, k, b)
        o2 = each(_bdot, att, v)
        kv = each(lambda x, y, z: _dot_tn(x, y * jnp.exp(z[C - 1:C, :] - z)), v, k, b)
        decay = each(lambda z: jnp.exp(z[C - 1:C, :]), b)
        st = [s_scr[hd] for hd in hs]
        for u in range(unroll):
            o1 = [_dot_nt(qd[u * heads + hd], st[hd]) for hd in hs]
            for hd in hs:
                i = u * heads + hd
                o_ref[pl.ds(t0 + u * C, C), hd * HG_HEAD:(hd + 1) * HG_HEAD] = o1[hd] + o2[i]
            st = [st[hd] * decay[u * heads + hd] + kv[u * heads + hd] for hd in hs]
        for hd in hs:
            s_scr[hd] = st[hd]
        return carry

    lax.fori_loop(0, tblk // (C * unroll), chunk, 0)

    @pl.when(ti == pl.num_programs(1) - 1)
    def _():
        sout_ref[...] = s_scr[...]


def _gla(q, k, v, lf, s0t, *, tblk, unroll=4):
    B, T, D = q.shape
    H = s0t.shape[1]
    seq = pl.BlockSpec((None, tblk, D), lambda bi, ti: (bi, ti, 0))
    st = pl.BlockSpec((None, H, HG_HEAD, HG_HEAD), lambda bi, ti: (bi, 0, 0, 0))
    return pl.pallas_call(
        functools.partial(_gla_kernel, tblk=tblk, heads=H, unroll=unroll),
        grid=(B, T // tblk), in_specs=[seq] * 4 + [st], out_specs=[seq, st],
        out_shape=[jax.ShapeDtypeStruct((B, T, D), F32), jax.ShapeDtypeStruct(s0t.shape, F32)],
        scratch_shapes=[pltpu.VMEM((H, HG_HEAD, HG_HEAD), F32)],
        compiler_params=_params(("arbitrary", "arbitrary")), name="gla_chunked")(q, k, v, lf, s0t)


def _gla_step_kernel(q_ref, k_ref, v_ref, lf_ref, s0_ref, o_ref, sout_ref, *, heads):
    n = HG_HEAD
    eye = jnp.where(_iota((n, n), 0) == _iota((n, n), 1), 1.0, 0.0)
    ones = jnp.ones((n, n), BF16)
    for hd in range(heads):
        sl = slice(hd * n, (hd + 1) * n)
        vcol = _sel_dot(eye * v_ref[:, sl], ones)
        st = s0_ref[hd] * jnp.exp(lf_ref[:, sl]) + vcol * k_ref[:, sl]
        sout_ref[hd] = st
        o_ref[:, sl] = _dot_nt(jnp.broadcast_to(q_ref[:, sl], (8, n)), st)[0:1, :]


def _gla_step(q, k, v, lf, s0t):
    B, _, D = q.shape
    H = s0t.shape[1]
    seq = pl.BlockSpec((None, 1, D), lambda bi: (bi, 0, 0))
    st = pl.BlockSpec((None, H, HG_HEAD, HG_HEAD), lambda bi: (bi, 0, 0, 0))
    return pl.pallas_call(
        functools.partial(_gla_step_kernel, heads=H), grid=(B,),
        in_specs=[seq] * 4 + [st], out_specs=[seq, st],
        out_shape=[jax.ShapeDtypeStruct((B, 1, D), F32), jax.ShapeDtypeStruct(s0t.shape, F32)],
        compiler_params=_params(("arbitrary",)), name="gla_step")(q, k, v, lf, s0t)


def _fox_proj_kernel(x_ref, gpre_ref, w_ref, wf_ref, bf_ref, qn_ref, kn_ref,
                     q_out, k_out, v_out, g_out, lf_out, c_out, carry_scr, *, seq_tiles, single_step):
    D = x_ref.shape[1]
    tm = x_ref.shape[0]
    h = _rms(x_ref[...], gpre_ref[...]).astype(BF16)
    proj = lambda n: jnp.dot(h, w_ref[:, n * D:(n + 1) * D], preferred_element_type=F32)
    head_norm = lambda t, gain: t * lax.rsqrt(_seg_sum(t * t, FX_HEAD) * (1.0 / FX_HEAD) + NORM_EPS) * gain
    q_out[...] = (head_norm(proj(0), qn_ref[...]) * (FX_HEAD ** -0.5)).astype(q_out.dtype)
    k_out[...] = head_norm(proj(1), kn_ref[...])
    v_out[...] = proj(2)
    g_out[...] = proj(3).astype(g_out.dtype)
    z = jnp.dot(h, wf_ref[...], preferred_element_type=F32) + bf_ref[...]
    lf = jnp.minimum(z, 0.0) - jnp.log1p(jnp.exp(-jnp.abs(z)))
    lf_out[...] = lf[:, :FX_HEADS]
    if single_step:
        c_out[...] = lf[:, :FX_HEADS]
    else:
        ti = pl.program_id(0) % seq_tiles

        @pl.when(ti == 0)
        def _():
            carry_scr[...] = jnp.zeros_like(carry_scr)

        tri = jnp.where(_iota((tm, tm), 0) >= _iota((tm, tm), 1), 1.0, 0.0).astype(BF16)
        c = _sel_dot_left(tri, lf) + carry_scr[...]
        carry_scr[...] = c[tm - 1:tm, :]
        c_out[...] = c[:, :FX_HEADS]


def _fox_proj(x, gpre, w_main, w_f, b_f, qn, kn, *, seq_len, tm, act_dtype=F32):
    M, D = x.shape
    single_step = seq_len == 1
    row = pl.BlockSpec((tm, D), lambda i: (i, 0))
    hrow = pl.BlockSpec((tm, FX_HEADS), lambda i: (i, 0))
    consts = [gpre, w_main, w_f, b_f, qn, kn]
    return pl.pallas_call(
        functools.partial(_fox_proj_kernel, seq_tiles=1 if single_step else seq_len // tm,
                          single_step=single_step),
        grid=(M // tm,), in_specs=[row] + [_const_spec(c.shape) for c in consts],
        out_specs=[row] * 4 + [hrow] * 2,
        out_shape=[jax.ShapeDtypeStruct((M, D), dt) for dt in (act_dtype, F32, F32, act_dtype)]
        + [jax.ShapeDtypeStruct((M, FX_HEADS), F32)] * 2,
        scratch_shapes=[pltpu.VMEM((1, LANES), F32)],
        compiler_params=_params(("arbitrary",)), name="fox_proj")(x, *consts)


def _fox_attn_kernel(q_ref, k_ref, vt_ref, cq_ref, ck_ref, o_ref, *, tq, tk, pairs):
    qi = pl.program_id(2)
    n = FX_HEAD
    heads = range(2 * pairs)
    first = _iota((1, LANES), 1) < n
    qh, cqh = [], []
    for p in range(pairs):
        q = q_ref[:, p * LANES:(p + 1) * LANES].astype(F32) * LOG2E
        qh += [jnp.where(first, q, 0.0).astype(BF16), jnp.where(first, 0.0, q).astype(BF16)]
        cq = cq_ref[p] * LOG2E
        cqh += [cq[0:1, :], cq[1:2, :]]
    full_blocks = (qi * tq) // tk

    def block(ki, carry, diagonal):
        k0 = pl.multiple_of(ki * tk, tk)
        kb = [k_ref[pl.ds(k0, tk), p * LANES:(p + 1) * LANES].astype(BF16) for p in range(pairs)]
        keep = None
        if diagonal:
            keep = (_iota((tk, tq), 0) - _iota((tk, tq), 1)) <= (qi * tq - k0)
        s = [_dot_nt(kb[h // 2], qh[h]) for h in heads]
        out = []
        probs = []
        for h in heads:
            m, l, _ = carry[h]
            ck = ck_ref[h // 2, pl.ds(k0, tk), :] * LOG2E
            sh = s[h] + cqh[h] - ck[:, h % 2:h % 2 + 1]
            if diagonal:
                sh = jnp.where(keep, sh, NEG_BIG)
            m_new = jnp.maximum(m, jnp.max(sh, axis=0, keepdims=True))
            p = jnp.exp2(sh - m_new)
            a = jnp.exp2(m - m_new)
            probs.append((m_new, l * a + jnp.sum(p, axis=0, keepdims=True), a, p.astype(BF16)))
        for h in heads:
            m_new, l_new, a, p = probs[h]
            vt = vt_ref[h // 2, ki, (h % 2) * n:(h % 2 + 1) * n, :].astype(BF16)
            out.append((m_new, l_new, carry[h][2] * a + jnp.dot(vt, p, preferred_element_type=F32)))
        return tuple(out)

    row = lambda val: jnp.full((1, tq), val, F32)
    init = tuple((row(NEG_BIG), row(0.0), jnp.zeros((n, tq), F32)) for _ in heads)
    carry = lax.fori_loop(0, full_blocks, lambda ki, c: block(ki, c, False), init)
    fin = block(full_blocks, carry, True)
    o_ref[...] = jnp.concatenate([acc / l for _, l, acc in fin], axis=0).T


def _fox_attn(q, k, v, c, *, tq, tk, pairs):
    B, T, D = q.shape
    hp = D // LANES
    W = pairs * LANES
    ck = jnp.transpose(c.reshape(B, T, hp, 2), (0, 2, 1, 3))
    cq = jnp.transpose(c.reshape(B, T, hp, 2), (0, 2, 3, 1))
    vt = jnp.transpose(v.reshape(B, T // tk, tk, hp, LANES), (0, 3, 1, 4, 2))
    qs = pl.BlockSpec((None, tq, W), lambda b, p, i: (b, i, p))
    ks = pl.BlockSpec((None, T, W), lambda b, p, i: (b, 0, p))
    vs = pl.BlockSpec((None, pairs, T // tk, LANES, tk), lambda b, p, i: (b, p, 0, 0, 0))
    cqs = pl.BlockSpec((None, pairs, 2, tq), lambda b, p, i: (b, p, 0, i))
    cks = pl.BlockSpec((None, pairs, T, 2), lambda b, p, i: (b, p, 0, 0))
    return pl.pallas_call(
        functools.partial(_fox_attn_kernel, tq=tq, tk=tk, pairs=pairs), grid=(B, hp // pairs, T // tq),
        in_specs=[qs, ks, vs, cqs, cks], out_specs=qs, out_shape=jax.ShapeDtypeStruct((B, T, D), F32),
        compiler_params=_params(("arbitrary",) * 3), name="fox_attn")(q, k, vt, cq, ck)


def _fox_paged_kernel(*refs, npg):
    it = iter(refs)
    _pt_ref = next(it)
    kt_refs, vt_refs, lf_refs = ([next(it) for _ in range(npg)] for _ in range(3))
    q_ref, kn_ref, vn_ref, cq_ref, o_ref, m_scr, l_scr, carry_scr, acc_scr = (next(it) for _ in range(9))
    g = pl.program_id(1)
    H, N, P = FX_HEADS, FX_HEAD, PAGE_SIZE
    D = H * N

    @pl.when(g == 0)
    def _():
        m_scr[...] = jnp.full_like(m_scr, NEG_BIG)
        l_scr[...] = jnp.zeros_like(l_scr)
        carry_scr[...] = jnp.zeros_like(carry_scr)
        acc_scr[...] = jnp.zeros_like(acc_scr)

    own = _iota((H, D), 0) == _iota((H, D), 1) // N
    qbd = jnp.where(own, q_ref[...], 0.0).astype(BF16)
    later = jnp.where(_iota((P, P), 0) > _iota((P, P), 1), 1.0, 0.0).astype(BF16)
    cq = cq_ref[...]
    carry = carry_scr[...]
    scores = [None] * npg
    for j in reversed(range(npg)):
        lf = lf_refs[j][...]
        s = jnp.dot(qbd, kt_refs[j][...].astype(BF16), preferred_element_type=F32)
        scores[j] = s + _sel_dot(lf, later) + (carry + cq)
        carry = carry + jnp.sum(lf, axis=1, keepdims=True)
    carry_scr[...] = carry
    m_old = m_scr[...]
    m_new = m_old
    for j in range(npg):
        m_new = jnp.maximum(m_new, jnp.max(scores[j], axis=1, keepdims=True))
    alpha = jnp.exp(m_old - m_new)
    l_new = l_scr[...] * alpha
    acc = acc_scr[...] * alpha
    for j in range(npg):
        p = jnp.exp(scores[j] - m_new)
        l_new = l_new + jnp.sum(p, axis=1, keepdims=True)
        acc = acc + _dot_nt(p, vt_refs[j][...])
    acc_scr[...] = acc
    m_scr[...] = m_new
    l_scr[...] = l_new

    @pl.when(g == pl.num_programs(1) - 1)
    def _():
        s_new = jnp.sum(jnp.where(own, q_ref[...] * kn_ref[...], 0.0), axis=1, keepdims=True)
        m_fin = jnp.maximum(m_new, s_new)
        a_fin = jnp.exp(m_new - m_fin)
        p_new = jnp.exp(s_new - m_fin)
        mix = (acc * a_fin + vn_ref[...] * p_new) / (l_new * a_fin + p_new)
        o_ref[...] = jnp.sum(jnp.where(own, mix, 0.0), axis=0, keepdims=True)


def _fox_paged(q, k_new, v_new, lf_new, cache_kt, cache_vt, cache_lft, page_table, *, npg):
    B, _, D = q.shape
    H, P = FX_HEADS, PAGE_SIZE
    n_pages = page_table.shape[1]
    groups = n_pages // npg

    def page(j):
        return lambda b, g, pt: (pt[b, (groups - 1 - g) * npg + j], 0, 0)

    kspecs = [pl.BlockSpec((None, D, P), page(j)) for j in range(npg)]
    lspecs = [pl.BlockSpec((None, H, P), page(j)) for j in range(npg)]
    tok = pl.BlockSpec((None, 1, D), lambda b, g, pt: (b, 0, 0))
    tokh = pl.BlockSpec((None, H, 1), lambda b, g, pt: (b, 0, 0))
    grid_spec = pltpu.PrefetchScalarGridSpec(
        num_scalar_prefetch=1, grid=(B, groups),
        in_specs=kspecs + kspecs + lspecs + [tok, tok, tok, tokh], out_specs=tok,
        scratch_shapes=[pltpu.VMEM((H, 1), F32), pltpu.VMEM((H, 1), F32), pltpu.VMEM((H, 1), F32),
                        pltpu.VMEM((H, D), F32)])
    return pl.pallas_call(
        functools.partial(_fox_paged_kernel, npg=npg), grid_spec=grid_spec,
        out_shape=jax.ShapeDtypeStruct((B, 1, D), F32),
        compiler_params=_params(("arbitrary", "arbitrary")), name="fox_paged")(
            page_table, *([cache_kt] * npg), *([cache_vt] * npg), *([cache_lft] * npg),
            q, k_new, v_new, lf_new.reshape(B, H, 1))


def _row_tile(m, want):
    return want if m % want == 0 else m


def kernel(x_prompt, x_sample, state_rwkv_wkv, state_rwkv_shift, state_hgrn, cache_fox_k, cache_fox_v, cache_fox_logf, page_table, norm_pre_mix, norm_post_mix, norm_pre_ffn, norm_post_ffn, rw_mu, rw_wr, rw_wk, rw_wv, rw_wo, rw_w0, rw_w1, rw_w2, rw_a0, rw_a1, rw_a2, rw_v0, rw_v1, rw_v2, rw_g1, rw_g2, rw_kk, rw_ka, rw_rk, rw_lnw, rw_lnb, hg_w_in, hg_lb, hg_norm, hg_wo, fx_w_in, fx_bf, fx_qn, fx_kn, fx_wo, ffn_wg, ffn_wu, ffn_wd):
    D = D_MODEL
    bf = lambda t: t.astype(BF16)
    vec = lambda t: t.reshape(1, -1).astype(F32)

    n_cache_pages = cache_fox_k.shape[1]
    cache_k = jnp.transpose(cache_fox_k, (0, 1, 3, 4, 2)).reshape(-1, D, PAGE_SIZE)
    cache_v = jnp.transpose(cache_fox_v, (0, 1, 3, 4, 2)).reshape(-1, D, PAGE_SIZE)
    cache_lf = jnp.transpose(cache_fox_logf, (0, 1, 3, 2)).reshape(-1, FX_HEADS, PAGE_SIZE)

    def trunk(x3, wkv0, shift0, hg0, paged):
        B, T, _ = x3.shape
        M = B * T
        x = x3.reshape(M, D)
        tm = _row_tile(M, 256)
        act = F32 if T == 1 else BF16
        v_first = None
        wkv_n, shift_n, hg_n, k_n, v_n, lf_n = [], [], [], [], [], []
        for i in range(DEPTH):
            j = i // N_MIXERS
            gpre, gpost = vec(norm_pre_mix[i]), vec(norm_post_mix[i])
            if i % N_MIXERS == 0:
                vecs = [rw_w0[j], rw_a0[j], rw_kk[j], rw_ka[j]]
                vecs += [rw_v0[j - 1]] if j > 0 else []
                vecs = jnp.stack(vecs + [jnp.zeros((D,), F32)] * (8 - len(vecs)))
                mats = [bf(t[j]) for t in (rw_wr, rw_wk, rw_wv, rw_w1, rw_w2, rw_a1, rw_a2, rw_g1, rw_g2)]
                vres = None if j == 0 else (bf(rw_v1[j - 1]), bf(rw_v2[j - 1]), v_first)
                r, w, k, v, kk, kb, g, shift = _rwkv_proj(x, shift0[j], gpre, rw_mu[j], vecs, mats, vres,
                                                          seq_len=T, tm=tm, act_dtype=act)
                if j == 0:
                    v_first = v
                seq = lambda t: t.reshape(B, T, D)
                wkv = _wkv_step if T == 1 else functools.partial(_wkv_chunked, tblk=min(T, 256))
                o, s_new = wkv(seq(r), seq(w), seq(k), seq(v), seq(kk), seq(kb), wkv0[j])
                ovec = jnp.stack([rw_rk[j].reshape(-1), rw_lnw[j], rw_lnb[j]] + [jnp.zeros((D,), F32)] * 5)
                x = _mixer_out(_rwkv_out_kernel, [o.reshape(M, D), r, k, v, g, x],
                               [ovec, bf(rw_wo[j]), gpost], tm=tm, name="rwkv_out")
                wkv_n.append(s_new)
                shift_n.append(shift)
            elif i % N_MIXERS == 1:
                q, k, v, lf, g = _hgrn_proj(x, gpre, hg_lb, bf(hg_w_in[j]), layer=i, tm=tm, act_dtype=act)
                seq = lambda t: t.reshape(B, T, D)
                s0t = jnp.swapaxes(hg0[j], -1, -2)
                if T == 1:
                    o, st = _gla_step(seq(q), seq(k), seq(v), seq(lf), s0t)
                else:
                    o, st = _gla(seq(q), seq(k), seq(v), seq(lf), s0t, tblk=min(T, 256))
                x = _mixer_out(_hgrn_out_kernel, [o.reshape(M, D), g, x],
                               [vec(hg_norm[j]), bf(hg_wo[j]), gpost], tm=tm, name="hgrn_out")
                hg_n.append(jnp.swapaxes(st, -1, -2))
            else:
                w_in = fx_w_in[j]
                w_f = jnp.pad(w_in[:, 4 * D:], ((0, 0), (0, LANES - FX_HEADS)))
                b_f = jnp.pad(fx_bf[j], (0, LANES - FX_HEADS)).reshape(1, LANES)
                tile = lambda t: jnp.tile(t, FX_HEADS).reshape(1, D)
                q, k, v, g, lf, c = _fox_proj(x, gpre, bf(w_in[:, :4 * D]), bf(w_f), b_f,
                                              tile(fx_qn[j]), tile(fx_kn[j]), seq_len=T, tm=tm, act_dtype=act)
                seq = lambda t: t.reshape(B, T, D)
                if paged:
                    pt = page_table + j * n_cache_pages
                    att = _fox_paged(seq(q), seq(k), seq(v), lf.reshape(B, 1, FX_HEADS),
                                     cache_k, cache_v, cache_lf, pt, npg=8)
                else:
                    att = _fox_attn(seq(q), seq(k), seq(v), c.reshape(B, T, FX_HEADS), tq=256, tk=512, pairs=2)
                x = _mixer_out(_fox_out_kernel, [att.reshape(M, D), g, x], [bf(fx_wo[j]), gpost],
                               tm=tm, name="fox_out")
                k_n.append(k.reshape(B, T, FX_HEADS, FX_HEAD))
                v_n.append(v.reshape(B, T, FX_HEADS, FX_HEAD))
                lf_n.append(lf.reshape(B, T, FX_HEADS))
            x = _ffn(x, vec(norm_pre_ffn[i]), vec(norm_post_ffn[i]), bf(ffn_wg[i]), bf(ffn_wu[i]),
                     bf(ffn_wd[i]), tm=tm)
        return (x.reshape(B, T, D), jnp.stack(wkv_n), jnp.stack(shift_n), jnp.stack(hg_n),
                jnp.stack(k_n), jnp.stack(v_n), jnp.stack(lf_n))

    bp = x_prompt.shape[0]
    n_a, n_b = state_rwkv_wkv.shape[0], state_hgrn.shape[0]
    y_p, wkv_p, shift_p, hg_p, k_p, v_p, lf_p = trunk(
        x_prompt, jnp.zeros((n_a, bp) + state_rwkv_wkv.shape[2:], F32), jnp.zeros((n_a, bp, D), F32),
        jnp.zeros((n_b, bp) + state_hgrn.shape[2:], F32), False)
    y_s, wkv_s, shift_s, hg_s, k_s, v_s, lf_s = trunk(
        x_sample, state_rwkv_wkv, state_rwkv_shift, state_hgrn, True)
    return (y_p, y_s, wkv_p, wkv_s, shift_p, shift_s, hg_p, hg_s, k_p, k_s, v_p, v_s, lf_p, lf_s)
```

```python
import functools
import math

import jax
import jax.numpy as jnp
from jax import lax
from jax.experimental import pallas as pl
from jax.experimental.pallas import tpu as pltpu

F32 = jnp.float32
BF16 = jnp.bfloat16

LANES = 128
VMEM_LIMIT = 56 * 1024 * 1024

D_MODEL = 1024
DEPTH = 4
N_MIXERS = 3
RW_HEAD = 64
RW_HEADS = D_MODEL // RW_HEAD
RW_DECAY_SCALE = 0.6065306597126334
RW_GN_EPS = 64e-5
HG_HEADS = 8
HG_HEAD = D_MODEL // HG_HEADS
HG_CHUNK = 16
FX_HEAD = 64
FX_HEADS = D_MODEL // FX_HEAD
PAGE_SIZE = 128
D_FF = 2816
NORM_EPS = 1e-6
NEG_BIG = -1e30
LOG2E = 1.4426950408889634


def _params(sem):
    return pltpu.CompilerParams(dimension_semantics=sem, vmem_limit_bytes=VMEM_LIMIT)


def _const_spec(shape):
    n = len(shape)
    return pl.BlockSpec(shape, lambda *_: (0,) * n, pipeline_mode=pl.Buffered(1))


def _rms(x, g, eps=NORM_EPS):
    return x * lax.rsqrt(jnp.mean(x * x, axis=-1, keepdims=True) + eps) * g


def _sigmoid(x):
    return 1.0 / (1.0 + jnp.exp(-x))


def _bdot(a, b):
    return jnp.dot(a.astype(BF16), b.astype(BF16), preferred_element_type=F32)


def _dot_nt(a, b):
    return lax.dot_general(a.astype(BF16), b.astype(BF16), (((1,), (1,)), ((), ())),
                           preferred_element_type=F32)


def _dot_tn(a, b):
    return lax.dot_general(a.astype(BF16), b.astype(BF16), (((0,), (0,)), ((), ())),
                           preferred_element_type=F32)


def _split(x):
    hi = x.astype(BF16)
    lo = (x - hi.astype(F32)).astype(BF16)
    return hi, lo


def _sel_dot(x, sel):
    hi, lo = _split(x)
    return (jnp.dot(hi, sel, preferred_element_type=F32)
            + jnp.dot(lo, sel, preferred_element_type=F32))


def _sel_dot_left(sel, x):
    hi, lo = _split(x)
    return (jnp.dot(sel, hi, preferred_element_type=F32)
            + jnp.dot(sel, lo, preferred_element_type=F32))


def _iota(shape, dim):
    return lax.broadcasted_iota(jnp.int32, shape, dim)


def _block_ones(n, seg):
    same = (_iota((n, n), 0) // seg) == (_iota((n, n), 1) // seg)
    return jnp.where(same, 1.0, 0.0).astype(BF16)


def _seg_sum(x, seg):
    sel = _block_ones(LANES, seg)
    parts = [_sel_dot(x[:, j * LANES:(j + 1) * LANES], sel) for j in range(x.shape[1] // LANES)]
    return jnp.concatenate(parts, axis=1)


def _rwkv_proj_kernel(*refs, seq_tiles, has_vres, single_step):
    it = iter(refs)
    x_ref, shift_ref, gpre_ref, mu_ref, vec_ref = (next(it) for _ in range(5))
    wr_ref, wk_ref, wv_ref, w1_ref, w2_ref, a1_ref, a2_ref, g1_ref, g2_ref = (next(it) for _ in range(9))
    if has_vres:
        v1_ref, v2_ref, vfirst_ref = (next(it) for _ in range(3))
    r_out, w_out, k_out, v_out, kk_out, b_out, g_out, shift_out = (next(it) for _ in range(8))
    last_scr = next(it)

    h = _rms(x_ref[...], gpre_ref[...])
    tm = h.shape[0]
    if single_step:
        prev = shift_ref[...]
        shift_out[...] = h
    else:
        i = pl.program_id(0)
        b = i // seq_tiles
        ti = i % seq_tiles
        first = jnp.where(ti == 0, shift_ref[pl.ds(b, 1), :], last_scr[...])
        prev = jnp.where(_iota((tm, 1), 0) == 0, first, pltpu.roll(h, 1, 0))
        last_scr[...] = h[tm - 1:tm, :]

        @pl.when(ti == seq_tiles - 1)
        def _():
            shift_out[pl.ds(b, 1), :] = h[tm - 1:tm, :]

    d = prev - h
    mix = lambda n: h + d * mu_ref[n:n + 1, :]
    w0, a0, k_k, k_a = (vec_ref[n:n + 1, :] for n in range(4))

    xv = mix(3)
    r = _bdot(mix(0), wr_ref[...])
    k = _bdot(mix(2), wk_ref[...])
    v = _bdot(xv, wv_ref[...])
    wl = w0 + _bdot(jnp.tanh(_bdot(mix(1), w1_ref[...])), w2_ref[...])
    a = _sigmoid(a0 + _bdot(_bdot(mix(4), a1_ref[...]), a2_ref[...]))
    if has_vres:
        gate = _sigmoid(vec_ref[4:5, :] + _bdot(_bdot(xv, v1_ref[...]), v2_ref[...]))
        v = v + (vfirst_ref[...].astype(F32) - v) * gate
    g = _bdot(_sigmoid(_bdot(mix(5), g1_ref[...])), g2_ref[...])

    kk = k * k_k
    kk = kk / jnp.maximum(jnp.sqrt(_seg_sum(kk * kk, RW_HEAD)), 1e-12)
    put = lambda ref, val: ref.__setitem__(Ellipsis, val.astype(ref.dtype))
    put(r_out, r)
    w_out[...] = -RW_DECAY_SCALE * _sigmoid(wl)
    put(k_out, k * (1.0 + (a - 1.0) * k_a))
    put(v_out, v)
    put(kk_out, kk)
    put(b_out, kk * a)
    put(g_out, g)


def _rwkv_proj(x, shift0, gpre, mu, vecs, mats, vres, *, seq_len, tm, act_dtype=F32):
    M, D = x.shape
    B = M // seq_len
    single_step = seq_len == 1
    seq_tiles = 1 if single_step else seq_len // tm
    row = pl.BlockSpec((tm, D), lambda i: (i, 0))
    shift_spec = row if single_step else _const_spec((B, D))
    shift_out_spec = row if single_step else pl.BlockSpec((B, D), lambda i: (0, 0))
    ins = [x, shift0, gpre, mu, vecs] + list(mats)
    specs = [row, shift_spec, _const_spec(gpre.shape), _const_spec(mu.shape), _const_spec(vecs.shape)]
    specs += [_const_spec(m.shape) for m in mats]
    if vres is not None:
        v1, v2, vfirst = vres
        ins += [v1, v2, vfirst]
        specs += [_const_spec(v1.shape), _const_spec(v2.shape), row]
    act = jax.ShapeDtypeStruct((M, D), act_dtype)
    outs = [act, jax.ShapeDtypeStruct((M, D), F32)] + [act] * 5 + [jax.ShapeDtypeStruct((B, D), F32)]
    return pl.pallas_call(
        functools.partial(_rwkv_proj_kernel, seq_tiles=seq_tiles, has_vres=vres is not None,
                          single_step=single_step),
        grid=(M // tm,), in_specs=specs, out_specs=[row] * 7 + [shift_out_spec], out_shape=outs,
        scratch_shapes=[pltpu.VMEM((1, D), F32)],
        compiler_params=_params(("arbitrary",)), name="rwkv_proj")(*ins)


def _wkv_step_kernel(r_ref, lw_ref, k_ref, v_ref, kk_ref, b_ref, s0_ref, o_ref, sout_ref, *, pairs):
    n = RW_HEAD
    ones2 = _block_ones(LANES, n)
    diag2 = jnp.where(_iota((n, LANES), 0) == (_iota((n, LANES), 1) % n), 1.0, 0.0)
    for p in range(pairs):
        sl = slice(p * LANES, (p + 1) * LANES)
        s = jnp.concatenate([s0_ref[2 * p], s0_ref[2 * p + 1]], axis=1)
        lhs = jnp.concatenate([s * kk_ref[:, sl], diag2 * v_ref[:, sl]], axis=0)
        res = jnp.dot(lhs.astype(BF16), ones2, preferred_element_type=F32)
        s = s * jnp.exp(lw_ref[:, sl]) - res[:n] * b_ref[:, sl] + res[n:] * k_ref[:, sl]
        ob = jnp.dot((s * r_ref[:, sl]).astype(BF16), ones2, preferred_element_type=F32)
        o_ref[:, sl] = jnp.sum(ob * diag2, axis=0, keepdims=True)
        sout_ref[2 * p] = s[:, :n]
        sout_ref[2 * p + 1] = s[:, n:]


def _wkv_step(r, lw, k, v, kk, b, s0):
    B, _, D = r.shape
    H = s0.shape[1]
    seq = pl.BlockSpec((None, 1, D), lambda bi: (bi, 0, 0))
    st = pl.BlockSpec((None, H, RW_HEAD, RW_HEAD), lambda bi: (bi, 0, 0, 0))
    return pl.pallas_call(
        functools.partial(_wkv_step_kernel, pairs=H // 2), grid=(B,),
        in_specs=[seq] * 6 + [st], out_specs=[seq, st],
        out_shape=[jax.ShapeDtypeStruct((B, 1, D), F32), jax.ShapeDtypeStruct(s0.shape, F32)],
        compiler_params=_params(("arbitrary",)), name="wkv_step")(r, lw, k, v, kk, b, s0)


WKV_CHUNK = 64
WKV_GROUP = 256 // RW_HEAD


def _wkv_chunk_kernel(r_ref, lw_ref, k_ref, v_ref, kk_ref, b_ref, s0_ref, o_ref, sout_ref, s_scr, *, tblk, groups,
                      unroll):
    C, W = WKV_CHUNK, 256
    ti = pl.program_id(1)

    @pl.when(ti == 0)
    def _():
        s_scr[...] = s0_ref[...]

    blk = jnp.where((_iota((W, W), 0) // C) == (_iota((W, W), 1) // C), 1.0, 0.0)
    blkb = blk.astype(BF16)
    tcol, trow = _iota((C, W), 1) % C, _iota((C, W), 0)
    strict, incl = tcol < trow, tcol <= trow
    eye = jnp.where(tcol == trow, 1.0, 0.0)
    tri = jnp.where(_iota((C, C), 1) <= _iota((C, C), 0), 1.0, 0.0).astype(BF16)

    def bd(x):
        return jnp.concatenate([x.astype(BF16)] * (W // C), axis=0) * blkb

    def sbs(y):
        ym = y * blk
        return (ym[0:C] + ym[C:2 * C]) + (ym[2 * C:3 * C] + ym[3 * C:4 * C])

    mm = lambda a, b: jnp.dot(a.astype(BF16), b, preferred_element_type=F32)

    def chunk(c, carry):
        t0 = pl.multiple_of(c * (C * unroll), C * unroll)
        G = range(groups)
        units = [(u, g) for u in range(unroll) for g in G]
        each = lambda f, *cols: [f(*args) for args in zip(*cols)]
        load = lambda ref: [ref[pl.ds(t0 + u * C, C), g * W:(g + 1) * W].astype(F32) for u, g in units]
        r, lw, k, v, kk, b = (load(ref) for ref in (r_ref, lw_ref, k_ref, v_ref, kk_ref, b_ref))
        cum = each(lambda x: _sel_dot_left(tri, x), lw)
        tot = each(lambda x: x[C - 1:C, :], cum)
        e_neg = each(lambda x: jnp.exp(-x), cum)
        rt = each(lambda x, y: x * jnp.exp(y), r, cum)
        at = each(lambda x, y, z: x * jnp.exp(y - z), kk, cum, lw)
        lhs2 = each(lambda x, y: jnp.concatenate([x, y], axis=0).astype(BF16), at, rt)
        gb = each(lambda l, x, e: _dot_nt(l, bd(x * e)), lhs2, b, e_neg)
        gk = each(lambda l, x, e: _dot_nt(l, bd(x * e)), lhs2, k, e_neg)
        aab = each(lambda x: jnp.where(strict, x[:C], 0.0), gb)
        arb = each(lambda x: jnp.where(incl, x[C:], 0.0).astype(BF16), gb)
        a2k = each(lambda x: jnp.concatenate([jnp.where(strict, x[:C], 0.0), jnp.where(incl, x[C:], 0.0)], axis=0), gk)
        av = each(lambda x, y: mm(x, bd(y)), a2k, v)
        x = each(lambda n: eye - n, aab)
        pw = aab
        pbd = each(bd, pw)
        for _ in range(5):
            pw = each(mm, pw, pbd)
            pbd = each(bd, pw)
            x = each(lambda y, z: y + mm(y, z), x, pbd)
        xb = each(lambda y: y.astype(BF16), x)
        p = each(lambda y, z: mm(y, bd(z)), xb, at)
        q = each(lambda y, z: mm(y, bd(z[:C])), xb, av)
        rp = each(lambda y, a, z: y - mm(a, bd(z)), rt, arb, p)
        op = each(lambda y, a, z: y[C:] - mm(a, bd(z)), av, arb, q)
        e_end = each(lambda t, y: jnp.exp(t - y), tot, cum)
        kp = each(lambda y, e: y * e, k, e_end)
        bp = each(lambda y, e: y * e, b, e_end)
        m_s = each(lambda t, y, z: eye * jnp.exp(t) - sbs(_dot_tn(y, z)), tot, bp, p)
        n_s = each(lambda y, z, vv, qq: sbs(_dot_tn(jnp.concatenate([y, z], axis=0),
                                                    jnp.concatenate([vv, -qq], axis=0))), kp, bp, v, q)
        state = [s_scr[g] for g in G]
        for u in range(unroll):
            sbd = each(bd, state)
            for g in G:
                i = u * groups + g
                o_ref[pl.ds(t0 + u * C, C), g * W:(g + 1) * W] = mm(rp[i], sbd[g]) + op[i]
            state = [mm(m_s[u * groups + g], sbd[g]) + n_s[u * groups + g] for g in G]
        for g in G:
            s_scr[g] = state[g]
        return carry

    lax.fori_loop(0, tblk // (C * unroll), chunk, 0)

    @pl.when(ti == pl.num_programs(1) - 1)
    def _():
        sout_ref[...] = s_scr[...]


def _wkv_chunked(r, lw, k, v, kk, b, s0, *, tblk, unroll=2):
    B, T, D = r.shape
    H, N = s0.shape[1], s0.shape[2]
    G = WKV_GROUP
    pack = lambda s: jnp.transpose(s.reshape(B, H // G, G, N, N), (0, 1, 4, 2, 3)).reshape(B, H // G, N, G * N)
    unpack = lambda s: jnp.transpose(s.reshape(B, H // G, N, G, N), (0, 1, 3, 4, 2)).reshape(B, H, N, N)
    seq = pl.BlockSpec((None, tblk, D), lambda bi, ti: (bi, ti, 0))
    st = pl.BlockSpec((None, H // G, N, G * N), lambda bi, ti: (bi, 0, 0, 0))
    o, s_new = pl.pallas_call(
        functools.partial(_wkv_chunk_kernel, tblk=tblk, groups=H // G, unroll=unroll),
        grid=(B, T // tblk), in_specs=[seq] * 6 + [st], out_specs=[seq, st],
        out_shape=[jax.ShapeDtypeStruct((B, T, D), F32), jax.ShapeDtypeStruct((B, H // G, N, G * N), F32)],
        scratch_shapes=[pltpu.VMEM((H // G, N, G * N), F32)],
        compiler_params=_params(("arbitrary", "arbitrary")), name="wkv_chunked")(r, lw, k, v, kk, b, pack(s0))
    return o, unpack(s_new)


def _finish(y, wo_ref, x_ref, gpost_ref, out_ref):
    z = jnp.dot(y.astype(BF16), wo_ref[...], preferred_element_type=F32)
    out_ref[...] = x_ref[...] + _rms(z, gpost_ref[...])


def _rwkv_out_kernel(o_ref, r_ref, k_ref, v_ref, g_ref, x_ref, vec_ref, wo_ref, gpost_ref, out_ref):
    rk, lnw, lnb = (vec_ref[n:n + 1, :] for n in range(3))
    o = o_ref[...]
    dev = o - _seg_sum(o, RW_HEAD) * (1.0 / RW_HEAD)
    var = _seg_sum(dev * dev, RW_HEAD) * (1.0 / RW_HEAD)
    on = dev * lax.rsqrt(var + RW_GN_EPS) * lnw + lnb
    f32 = lambda ref: ref[...].astype(F32)
    bonus = _seg_sum(f32(r_ref) * f32(k_ref) * rk, RW_HEAD) * f32(v_ref)
    _finish((on + bonus) * f32(g_ref), wo_ref, x_ref, gpost_ref, out_ref)


def _hgrn_out_kernel(o_ref, g_ref, x_ref, ng_ref, wo_ref, gpost_ref, out_ref):
    o = o_ref[...]
    parts = []
    for j in range(o.shape[1] // HG_HEAD):
        oj = o[:, j * HG_HEAD:(j + 1) * HG_HEAD]
        parts.append(oj * lax.rsqrt(jnp.mean(oj * oj, axis=-1, keepdims=True) + NORM_EPS))
    g = g_ref[...].astype(F32)
    y = jnp.concatenate(parts, axis=1) * ng_ref[...] * (g * _sigmoid(g))
    _finish(y, wo_ref, x_ref, gpost_ref, out_ref)


def _fox_out_kernel(o_ref, g_ref, x_ref, wo_ref, gpost_ref, out_ref):
    _finish(o_ref[...] * _sigmoid(g_ref[...].astype(F32)), wo_ref, x_ref, gpost_ref, out_ref)


def _mixer_out(body, rows, consts, *, tm, name):
    M, D = rows[0].shape
    row = pl.BlockSpec((tm, D), lambda i: (i, 0))
    return pl.pallas_call(
        body, grid=(M // tm,), in_specs=[row] * len(rows) + [_const_spec(c.shape) for c in consts],
        out_specs=row, out_shape=jax.ShapeDtypeStruct((M, D), F32),
        compiler_params=_params(("arbitrary",)), name=name)(*rows, *consts)


def _ffn_kernel(x_ref, gpre_ref, gpost_ref, wg_ref, wu_ref, wd_ref, out_ref):
    x = x_ref[...]
    h = _rms(x, gpre_ref[...]).astype(BF16)
    a = jnp.dot(h, wg_ref[...], preferred_element_type=F32)
    u = jnp.dot(h, wu_ref[...], preferred_element_type=F32)
    act = (a * _sigmoid(a) * u).astype(BF16)
    z = jnp.dot(act, wd_ref[...], preferred_element_type=F32)
    out_ref[...] = x + _rms(z, gpost_ref[...])


def _ffn(x, gpre, gpost, wg, wu, wd, *, tm):
    M, D = x.shape
    row = pl.BlockSpec((tm, D), lambda i: (i, 0))
    consts = [gpre, gpost, wg, wu, wd]
    return pl.pallas_call(
        _ffn_kernel, grid=(M // tm,), in_specs=[row] + [_const_spec(c.shape) for c in consts],
        out_specs=row, out_shape=jax.ShapeDtypeStruct((M, D), F32),
        compiler_params=_params(("arbitrary",)), name="ffn")(x, *consts)


def _hgrn_proj_kernel(x_ref, gpre_ref, lb_ref, w_ref, q_out, k_out, v_out, lf_out, g_out, *, layer):
    D = x_ref.shape[1]
    h = _rms(x_ref[...], gpre_ref[...]).astype(BF16)
    proj = lambda n: jnp.dot(h, w_ref[:, n * D:(n + 1) * D], preferred_element_type=F32)
    lbw = lb_ref[...]
    e = jnp.exp(lbw - jnp.max(lbw, axis=0, keepdims=True))
    lb = jnp.sum(e[1:layer + 1], axis=0, keepdims=True) / jnp.sum(e, axis=0, keepdims=True)
    q = proj(0)
    f = lb + (1.0 - lb) * _sigmoid(proj(1))
    q_out[...] = (q * _sigmoid(q)).astype(q_out.dtype)
    k_out[...] = (1.0 - f).astype(k_out.dtype)
    lf_out[...] = jnp.log(f)
    v_out[...] = proj(2).astype(v_out.dtype)
    g_out[...] = proj(3).astype(g_out.dtype)


def _hgrn_proj(x, gpre, hg_lb, w_in, *, layer, tm, act_dtype=F32):
    M, D = x.shape
    row = pl.BlockSpec((tm, D), lambda i: (i, 0))
    return pl.pallas_call(
        functools.partial(_hgrn_proj_kernel, layer=layer), grid=(M // tm,),
        in_specs=[row, _const_spec(gpre.shape), _const_spec(hg_lb.shape), _const_spec(w_in.shape)],
        out_specs=[row] * 5,
        out_shape=[jax.ShapeDtypeStruct((M, D), dt) for dt in (act_dtype, act_dtype, act_dtype, F32, act_dtype)],
        compiler_params=_params(("arbitrary",)), name="hgrn_proj")(x, gpre, hg_lb, w_in)


def _gla_kernel(q_ref, k_ref, v_ref, lf_ref, s0_ref, o_ref, sout_ref, s_scr, *, tblk, heads, unroll):
    ti = pl.program_id(1)
    C = HG_CHUNK

    @pl.when(ti == 0)
    def _():
        s_scr[...] = s0_ref[...]

    lower = _iota((C, C), 0) >= _iota((C, C), 1)
    tri = jnp.where(lower, 1.0, 0.0).astype(BF16)

    def chunk(c, carry):
        t0 = pl.multiple_of(c * (C * unroll), C * unroll)
        hs = range(heads)
        units = [(u, hd) for u in range(unroll) for hd in hs]
        each = lambda f, *cols: [f(*args) for args in zip(*cols)]
        load = lambda ref: [ref[pl.ds(t0 + u * C, C), hd * HG_HEAD:(hd + 1) * HG_HEAD].astype(F32)
                            for u, hd in units]
        q, k, v, lf = load(q_ref), load(k_ref), load(v_ref), load(lf_ref)
        b = each(lambda x: _sel_dot_left(tri, x), lf)
        qd = each(lambda x, y: (x * jnp.exp(y)).astype(BF16), q, b)
        att = each(lambda x, y, z: jnp.where(lower, _dot_nt(x, y * jnp.exp(-z)), 0.0), qd, k, b)
        o2 = each(_bdot, att, v)
        kv = each(lambda x, y, z: _dot_tn(x, y * jnp.exp(z[C - 1:C, :] - z)), v, k, b)
        decay = each(lambda z: jnp.exp(z[C - 1:C, :]), b)
        st = [s_scr[hd] for hd in hs]
        for u in range(unroll):
            o1 = [_dot_nt(qd[u * heads + hd], st[hd]) for hd in hs]
            for hd in hs:
                i = u * heads + hd
                o_ref[pl.ds(t0 + u * C, C), hd * HG_HEAD:(hd + 1) * HG_HEAD] = o1[hd] + o2[i]
            st = [st[hd] * decay[u * heads + hd] + kv[u * heads + hd] for hd in hs]
        for hd in hs:
            s_scr[hd] = st[hd]
        return carry

    lax.fori_loop(0, tblk // (C * unroll), chunk, 0)

    @pl.when(ti == pl.num_programs(1) - 1)
    def _():
        sout_ref[...] = s_scr[...]


def _gla(q, k, v, lf, s0t, *, tblk, unroll=4):
    B, T, D = q.shape
    H = s0t.shape[1]
    seq = pl.BlockSpec((None, tblk, D), lambda bi, ti: (bi, ti, 0))
    st = pl.BlockSpec((None, H, HG_HEAD, HG_HEAD), lambda bi, ti: (bi, 0, 0, 0))
    return pl.pallas_call(
        functools.partial(_gla_kernel, tblk=tblk, heads=H, unroll=unroll),
        grid=(B, T // tblk), in_specs=[seq] * 4 + [st], out_specs=[seq, st],
        out_shape=[jax.ShapeDtypeStruct((B, T, D), F32), jax.ShapeDtypeStruct(s0t.shape, F32)],
        scratch_shapes=[pltpu.VMEM((H, HG_HEAD, HG_HEAD), F32)],
        compiler_params=_params(("arbitrary", "arbitrary")), name="gla_chunked")(q, k, v, lf, s0t)


def _gla_step_kernel(q_ref, k_ref, v_ref, lf_ref, s0_ref, o_ref, sout_ref, *, heads):
    n = HG_HEAD
    eye = jnp.where(_iota((n, n), 0) == _iota((n, n), 1), 1.0, 0.0)
    ones = jnp.ones((n, n), BF16)
    for hd in range(heads):
        sl = slice(hd * n, (hd + 1) * n)
        vcol = _sel_dot(eye * v_ref[:, sl], ones)
        st = s0_ref[hd] * jnp.exp(lf_ref[:, sl]) + vcol * k_ref[:, sl]
        sout_ref[hd] = st
        o_ref[:, sl] = _dot_nt(jnp.broadcast_to(q_ref[:, sl], (8, n)), st)[0:1, :]


def _gla_step(q, k, v, lf, s0t):
    B, _, D = q.shape
    H = s0t.shape[1]
    seq = pl.BlockSpec((None, 1, D), lambda bi: (bi, 0, 0))
    st = pl.BlockSpec((None, H, HG_HEAD, HG_HEAD), lambda bi: (bi, 0, 0, 0))
    return pl.pallas_call(
        functools.partial(_gla_step_kernel, heads=H), grid=(B,),
        in_specs=[seq] * 4 + [st], out_specs=[seq, st],
        out_shape=[jax.ShapeDtypeStruct((B, 1, D), F32), jax.ShapeDtypeStruct(s0t.shape, F32)],
        compiler_params=_params(("arbitrary",)), name="gla_step")(q, k, v, lf, s0t)


def _fox_proj_kernel(x_ref, gpre_ref, w_ref, wf_ref, bf_ref, qn_ref, kn_ref,
                     q_out, k_out, v_out, g_out, lf_out, c_out, *rest, seq_tiles, single_step):
    carry_scr = rest[-1]
    D = x_ref.shape[1]
    tm = x_ref.shape[0]
    h = _rms(x_ref[...], gpre_ref[...]).astype(BF16)
    proj = lambda n: jnp.dot(h, w_ref[:, n * D:(n + 1) * D], preferred_element_type=F32)
    head_norm = lambda t, gain: t * lax.rsqrt(_seg_sum(t * t, FX_HEAD) * (1.0 / FX_HEAD) + NORM_EPS) * gain
    q_out[...] = (head_norm(proj(0), qn_ref[...]) * (FX_HEAD ** -0.5)).astype(q_out.dtype)
    k = head_norm(proj(1), kn_ref[...])
    v = proj(2)
    k_out[...] = k
    v_out[...] = v
    if not single_step:
        kt_out, vt_out = rest[0], rest[1]
        kt_out[...] = k.T
        vt_out[...] = v.T
    g_out[...] = proj(3).astype(g_out.dtype)
    z = jnp.dot(h, wf_ref[...], preferred_element_type=F32) + bf_ref[...]
    lf = jnp.minimum(z, 0.0) - jnp.log1p(jnp.exp(-jnp.abs(z)))
    lf_out[...] = lf[:, :FX_HEADS]
    if single_step:
        c_out[...] = lf[:, :FX_HEADS]
    else:
        ti = pl.program_id(0) % seq_tiles

        @pl.when(ti == 0)
        def _():
            carry_scr[...] = jnp.zeros_like(carry_scr)

        tri = jnp.where(_iota((tm, tm), 0) >= _iota((tm, tm), 1), 1.0, 0.0).astype(BF16)
        c = _sel_dot_left(tri, lf) + carry_scr[...]
        carry_scr[...] = c[tm - 1:tm, :]
        c_out[...] = c[:, :FX_HEADS]


def _fox_proj(x, gpre, w_main, w_f, b_f, qn, kn, *, seq_len, tm, act_dtype=F32):
    M, D = x.shape
    single_step = seq_len == 1
    row = pl.BlockSpec((tm, D), lambda i: (i, 0))
    hrow = pl.BlockSpec((tm, FX_HEADS), lambda i: (i, 0))
    consts = [gpre, w_main, w_f, b_f, qn, kn]
    seq_tiles = 1 if single_step else seq_len // tm
    out_specs = [row] * 4 + [hrow] * 2
    out_shape = ([jax.ShapeDtypeStruct((M, D), dt) for dt in (act_dtype, F32, F32, act_dtype)]
                 + [jax.ShapeDtypeStruct((M, FX_HEADS), F32)] * 2)
    if not single_step:
        tspec = pl.BlockSpec((None, D, tm), lambda i: (i // seq_tiles, 0, i % seq_tiles))
        out_specs += [tspec] * 2
        out_shape += [jax.ShapeDtypeStruct((M // seq_len, D, seq_len), F32)] * 2
    return pl.pallas_call(
        functools.partial(_fox_proj_kernel, seq_tiles=seq_tiles, single_step=single_step),
        grid=(M // tm,), in_specs=[row] + [_const_spec(c.shape) for c in consts],
        out_specs=out_specs, out_shape=out_shape,
        scratch_shapes=[pltpu.VMEM((1, LANES), F32)],
        compiler_params=_params(("arbitrary",)), name="fox_proj")(x, *consts)


def _fox_attn_kernel(q_ref, k_ref, vt_ref, cq_ref, ck_ref, o_ref, *, tq, tk, pairs):
    qi = pl.program_id(2)
    n = FX_HEAD
    heads = range(2 * pairs)
    first = _iota((1, LANES), 1) < n
    qh, cqh = [], []
    for p in range(pairs):
        q = q_ref[:, p * LANES:(p + 1) * LANES].astype(F32) * LOG2E
        qh += [jnp.where(first, q, 0.0).astype(BF16), jnp.where(first, 0.0, q).astype(BF16)]
        cq = cq_ref[p] * LOG2E
        cqh += [cq[0:1, :], cq[1:2, :]]
    full_blocks = (qi * tq) // tk

    def block(ki, carry, diagonal):
        k0 = pl.multiple_of(ki * tk, tk)
        kb = [k_ref[pl.ds(k0, tk), p * LANES:(p + 1) * LANES].astype(BF16) for p in range(pairs)]
        keep = None
        if diagonal:
            keep = (_iota((tk, tq), 0) - _iota((tk, tq), 1)) <= (qi * tq - k0)
        s = [_dot_nt(kb[h // 2], qh[h]) for h in heads]
        out = []
        probs = []
        for h in heads:
            m, l, _ = carry[h]
            ck = ck_ref[h // 2, pl.ds(k0, tk), :] * LOG2E
            sh = s[h] + cqh[h] - ck[:, h % 2:h % 2 + 1]
            if diagonal:
                sh = jnp.where(keep, sh, NEG_BIG)
            m_new = jnp.maximum(m, jnp.max(sh, axis=0, keepdims=True))
            p = jnp.exp2(sh - m_new)
            a = jnp.exp2(m - m_new)
            probs.append((m_new, l * a + jnp.sum(p, axis=0, keepdims=True), a, p.astype(BF16)))
        for h in heads:
            m_new, l_new, a, p = probs[h]
            vt = vt_ref[h * n:(h + 1) * n, pl.ds(k0, tk)].astype(BF16)
            out.append((m_new, l_new, carry[h][2] * a + jnp.dot(vt, p, preferred_element_type=F32)))
        return tuple(out)

    row = lambda val: jnp.full((1, tq), val, F32)
    init = tuple((row(NEG_BIG), row(0.0), jnp.zeros((n, tq), F32)) for _ in heads)
    carry = lax.fori_loop(0, full_blocks, lambda ki, c: block(ki, c, False), init)
    fin = block(full_blocks, carry, True)
    o_ref[...] = jnp.concatenate([acc / l for _, l, acc in fin], axis=0).T


def _fox_attn(q, k, vt, c, *, tq, tk, pairs):
    B, T, D = q.shape
    hp = D // LANES
    W = pairs * LANES
    ck = jnp.transpose(c.reshape(B, T, hp, 2), (0, 2, 1, 3))
    cq = jnp.transpose(c.reshape(B, T, hp, 2), (0, 2, 3, 1))
    qs = pl.BlockSpec((None, tq, W), lambda b, p, i: (b, i, p))
    ks = pl.BlockSpec((None, T, W), lambda b, p, i: (b, 0, p))
    vs = pl.BlockSpec((None, W, T), lambda b, p, i: (b, p, 0))
    cqs = pl.BlockSpec((None, pairs, 2, tq), lambda b, p, i: (b, p, 0, i))
    cks = pl.BlockSpec((None, pairs, T, 2), lambda b, p, i: (b, p, 0, 0))
    return pl.pallas_call(
        functools.partial(_fox_attn_kernel, tq=tq, tk=tk, pairs=pairs), grid=(B, hp // pairs, T // tq),
        in_specs=[qs, ks, vs, cqs, cks], out_specs=qs, out_shape=jax.ShapeDtypeStruct((B, T, D), F32),
        compiler_params=_params(("arbitrary",) * 3), name="fox_attn")(q, k, vt, cq, ck)


def _fox_paged_kernel(*refs, npg):
    it = iter(refs)
    _pt_ref = next(it)
    kt_refs, vt_refs, lf_refs = ([next(it) for _ in range(npg)] for _ in range(3))
    q_ref, kn_ref, vn_ref, cq_ref, o_ref, m_scr, l_scr, carry_scr, acc_scr = (next(it) for _ in range(9))
    g = pl.program_id(1)
    H, N, P = FX_HEADS, FX_HEAD, PAGE_SIZE
    D = H * N

    @pl.when(g == 0)
    def _():
        m_scr[...] = jnp.full_like(m_scr, NEG_BIG)
        l_scr[...] = jnp.zeros_like(l_scr)
        carry_scr[...] = jnp.zeros_like(carry_scr)
        acc_scr[...] = jnp.zeros_like(acc_scr)

    own = _iota((H, D), 0) == _iota((H, D), 1) // N
    qbd = jnp.where(own, q_ref[...], 0.0).astype(BF16)
    later = jnp.where(_iota((P, P), 0) > _iota((P, P), 1), 1.0, 0.0).astype(BF16)
    cq = cq_ref[...]
    carry = carry_scr[...]
    scores = [None] * npg
    for j in reversed(range(npg)):
        lf = lf_refs[j][...]
        s = jnp.dot(qbd, kt_refs[j][...].astype(BF16), preferred_element_type=F32)
        scores[j] = s + _sel_dot(lf, later) + (carry + cq)
        carry = carry + jnp.sum(lf, axis=1, keepdims=True)
    carry_scr[...] = carry
    m_old = m_scr[...]
    m_new = m_old
    for j in range(npg):
        m_new = jnp.maximum(m_new, jnp.max(scores[j], axis=1, keepdims=True))
    alpha = jnp.exp(m_old - m_new)
    l_new = l_scr[...] * alpha
    acc = acc_scr[...] * alpha
    for j in range(npg):
        p = jnp.exp(scores[j] - m_new)
        l_new = l_new + jnp.sum(p, axis=1, keepdims=True)
        acc = acc + _dot_nt(p, vt_refs[j][...])
    acc_scr[...] = acc
    m_scr[...] = m_new
    l_scr[...] = l_new

    @pl.when(g == pl.num_programs(1) - 1)
    def _():
        s_new = jnp.sum(jnp.where(own, q_ref[...] * kn_ref[...], 0.0), axis=1, keepdims=True)
        m_fin = jnp.maximum(m_new, s_new)
        a_fin = jnp.exp(m_new - m_fin)
        p_new = jnp.exp(s_new - m_fin)
        mix = (acc * a_fin + vn_ref[...] * p_new) / (l_new * a_fin + p_new)
        o_ref[...] = jnp.sum(jnp.where(own, mix, 0.0), axis=0, keepdims=True)


def _fox_paged(q, k_new, v_new, lf_new, cache_kt, cache_vt, cache_lft, page_table, *, npg):
    B, _, D = q.shape
    H, P = FX_HEADS, PAGE_SIZE
    n_pages = page_table.shape[1]
    groups = n_pages // npg

    def page(j):
        return lambda b, g, pt: (pt[b, (groups - 1 - g) * npg + j], 0, 0)

    kspecs = [pl.BlockSpec((None, D, P), page(j)) for j in range(npg)]
    lspecs = [pl.BlockSpec((None, H, P), page(j)) for j in range(npg)]
    tok = pl.BlockSpec((None, 1, D), lambda b, g, pt: (b, 0, 0))
    tokh = pl.BlockSpec((None, H, 1), lambda b, g, pt: (b, 0, 0))
    grid_spec = pltpu.PrefetchScalarGridSpec(
        num_scalar_prefetch=1, grid=(B, groups),
        in_specs=kspecs + kspecs + lspecs + [tok, tok, tok, tokh], out_specs=tok,
        scratch_shapes=[pltpu.VMEM((H, 1), F32), pltpu.VMEM((H, 1), F32), pltpu.VMEM((H, 1), F32),
                        pltpu.VMEM((H, D), F32)])
    return pl.pallas_call(
        functools.partial(_fox_paged_kernel, npg=npg), grid_spec=grid_spec,
        out_shape=jax.ShapeDtypeStruct((B, 1, D), F32),
        compiler_params=_params(("arbitrary", "arbitrary")), name="fox_paged")(
            page_table, *([cache_kt] * npg), *([cache_vt] * npg), *([cache_lft] * npg),
            q, k_new, v_new, lf_new.reshape(B, H, 1))


def _row_tile(m, want):
    return want if m % want == 0 else m


def kernel(x_prompt, x_sample, state_rwkv_wkv, state_rwkv_shift, state_hgrn, cache_fox_k, cache_fox_v, cache_fox_logf, page_table, norm_pre_mix, norm_post_mix, norm_pre_ffn, norm_post_ffn, rw_mu, rw_wr, rw_wk, rw_wv, rw_wo, rw_w0, rw_w1, rw_w2, rw_a0, rw_a1, rw_a2, rw_v0, rw_v1, rw_v2, rw_g1, rw_g2, rw_kk, rw_ka, rw_rk, rw_lnw, rw_lnb, hg_w_in, hg_lb, hg_norm, hg_wo, fx_w_in, fx_bf, fx_qn, fx_kn, fx_wo, ffn_wg, ffn_wu, ffn_wd):
    D = D_MODEL
    bf = lambda t: t.astype(BF16)
    vec = lambda t: t.reshape(1, -1).astype(F32)

    n_cache_pages = cache_fox_k.shape[1]
    cache_k = jnp.transpose(cache_fox_k, (0, 1, 3, 4, 2)).reshape(-1, D, PAGE_SIZE)
    cache_v = jnp.transpose(cache_fox_v, (0, 1, 3, 4, 2)).reshape(-1, D, PAGE_SIZE)
    cache_lf = jnp.transpose(cache_fox_logf, (0, 1, 3, 2)).reshape(-1, FX_HEADS, PAGE_SIZE)

    def trunk(x3, wkv0, shift0, hg0, paged):
        B, T, _ = x3.shape
        M = B * T
        x = x3.reshape(M, D)
        tm = _row_tile(M, 256)
        act = F32 if T == 1 else BF16
        v_first = None
        wkv_n, shift_n, hg_n, k_n, v_n, lf_n = [], [], [], [], [], []
        for i in range(DEPTH):
            j = i // N_MIXERS
            gpre, gpost = vec(norm_pre_mix[i]), vec(norm_post_mix[i])
            if i % N_MIXERS == 0:
                vecs = [rw_w0[j], rw_a0[j], rw_kk[j], rw_ka[j]]
                vecs += [rw_v0[j - 1]] if j > 0 else []
                vecs = jnp.stack(vecs + [jnp.zeros((D,), F32)] * (8 - len(vecs)))
                mats = [bf(t[j]) for t in (rw_wr, rw_wk, rw_wv, rw_w1, rw_w2, rw_a1, rw_a2, rw_g1, rw_g2)]
                vres = None if j == 0 else (bf(rw_v1[j - 1]), bf(rw_v2[j - 1]), v_first)
                r, w, k, v, kk, kb, g, shift = _rwkv_proj(x, shift0[j], gpre, rw_mu[j], vecs, mats, vres,
                                                          seq_len=T, tm=tm, act_dtype=act)
                if j == 0:
                    v_first = v
                seq = lambda t: t.reshape(B, T, D)
                wkv = _wkv_step if T == 1 else functools.partial(_wkv_chunked, tblk=min(T, 256))
                o, s_new = wkv(seq(r), seq(w), seq(k), seq(v), seq(kk), seq(kb), wkv0[j])
                ovec = jnp.stack([rw_rk[j].reshape(-1), rw_lnw[j], rw_lnb[j]] + [jnp.zeros((D,), F32)] * 5)
                x = _mixer_out(_rwkv_out_kernel, [o.reshape(M, D), r, k, v, g, x],
                               [ovec, bf(rw_wo[j]), gpost], tm=tm, name="rwkv_out")
                wkv_n.append(s_new)
                shift_n.append(shift)
            elif i % N_MIXERS == 1:
                q, k, v, lf, g = _hgrn_proj(x, gpre, hg_lb, bf(hg_w_in[j]), layer=i, tm=tm, act_dtype=act)
                seq = lambda t: t.reshape(B, T, D)
                s0t = jnp.swapaxes(hg0[j], -1, -2)
                if T == 1:
                    o, st = _gla_step(seq(q), seq(k), seq(v), seq(lf), s0t)
                else:
                    o, st = _gla(seq(q), seq(k), seq(v), seq(lf), s0t, tblk=min(T, 256))
                x = _mixer_out(_hgrn_out_kernel, [o.reshape(M, D), g, x],
                               [vec(hg_norm[j]), bf(hg_wo[j]), gpost], tm=tm, name="hgrn_out")
                hg_n.append(jnp.swapaxes(st, -1, -2))
            else:
                w_in = fx_w_in[j]
                w_f = jnp.pad(w_in[:, 4 * D:], ((0, 0), (0, LANES - FX_HEADS)))
                b_f = jnp.pad(fx_bf[j], (0, LANES - FX_HEADS)).reshape(1, LANES)
                tile = lambda t: jnp.tile(t, FX_HEADS).reshape(1, D)
                q, k, v, g, lf, c, *kvt = _fox_proj(x, gpre, bf(w_in[:, :4 * D]), bf(w_f), b_f, tile(fx_qn[j]),
                                                    tile(fx_kn[j]), seq_len=T, tm=tm, act_dtype=act)
                seq = lambda t: t.reshape(B, T, D)
                if paged:
                    pt = page_table + j * n_cache_pages
                    att = _fox_paged(seq(q), seq(k), seq(v), lf.reshape(B, 1, FX_HEADS),
                                     cache_k, cache_v, cache_lf, pt, npg=8)
                    k_n.append(k.reshape(B, T, FX_HEADS, FX_HEAD))
                    v_n.append(v.reshape(B, T, FX_HEADS, FX_HEAD))
                else:
                    kt, vt = kvt
                    att = _fox_attn(seq(q), seq(k), vt, c.reshape(B, T, FX_HEADS), tq=512, tk=512, pairs=2)
                    heads_last = lambda t: jnp.transpose(t.reshape(B, FX_HEADS, FX_HEAD, T), (0, 3, 1, 2))
                    k_n.append(heads_last(kt))
                    v_n.append(heads_last(vt))
                x = _mixer_out(_fox_out_kernel, [att.reshape(M, D), g, x], [bf(fx_wo[j]), gpost],
                               tm=tm, name="fox_out")
                lf_n.append(lf.reshape(B, T, FX_HEADS))
            x = _ffn(x, vec(norm_pre_ffn[i]), vec(norm_post_ffn[i]), bf(ffn_wg[i]), bf(ffn_wu[i]),
                     bf(ffn_wd[i]), tm=tm)
        return (x.reshape(B, T, D), jnp.stack(wkv_n), jnp.stack(shift_n), jnp.stack(hg_n),
                jnp.stack(k_n), jnp.stack(v_n), jnp.stack(lf_n))

    bp = x_prompt.shape[0]
    n_a, n_b = state_rwkv_wkv.shape[0], state_hgrn.shape[0]
    y_p, wkv_p, shift_p, hg_p, k_p, v_p, lf_p = trunk(
        x_prompt, jnp.zeros((n_a, bp) + state_rwkv_wkv.shape[2:], F32), jnp.zeros((n_a, bp, D), F32),
        jnp.zeros((n_b, bp) + state_hgrn.shape[2:], F32), False)
    y_s, wkv_s, shift_s, hg_s, k_s, v_s, lf_s = trunk(
        x_sample, state_rwkv_wkv, state_rwkv_shift, state_hgrn, True)
    return (y_p, y_s, wkv_p, wkv_s, shift_p, shift_s, hg_p, hg_s, k_p, k_s, v_p, v_s, lf_p, lf_s)
```

```python
import functools
import math

import jax
import jax.numpy as jnp
from jax import lax
from jax.experimental import pallas as pl
from jax.experimental.pallas import tpu as pltpu

F32 = jnp.float32
BF16 = jnp.bfloat16

LANES = 128
VMEM_LIMIT = 56 * 1024 * 1024

D_MODEL = 1024
DEPTH = 4
N_MIXERS = 3
RW_HEAD = 64
RW_HEADS = D_MODEL // RW_HEAD
RW_DECAY_SCALE = 0.6065306597126334
RW_GN_EPS = 64e-5
HG_HEADS = 8
HG_HEAD = D_MODEL // HG_HEADS
HG_CHUNK = 16
FX_HEAD = 64
FX_HEADS = D_MODEL // FX_HEAD
PAGE_SIZE = 128
D_FF = 2816
NORM_EPS = 1e-6
NEG_BIG = -1e30
LOG2E = 1.4426950408889634


def _params(sem):
    return pltpu.CompilerParams(dimension_semantics=sem, vmem_limit_bytes=VMEM_LIMIT)


def _const_spec(shape):
    n = len(shape)
    return pl.BlockSpec(shape, lambda *_: (0,) * n, pipeline_mode=pl.Buffered(1))


def _rms(x, g, eps=NORM_EPS):
    return x * lax.rsqrt(jnp.mean(x * x, axis=-1, keepdims=True) + eps) * g


def _sigmoid(x):
    return 1.0 / (1.0 + jnp.exp(-x))


def _bdot(a, b):
    return jnp.dot(a.astype(BF16), b.astype(BF16), preferred_element_type=F32)


def _dot_nt(a, b):
    return lax.dot_general(a.astype(BF16), b.astype(BF16), (((1,), (1,)), ((), ())),
                           preferred_element_type=F32)


def _dot_tn(a, b):
    return lax.dot_general(a.astype(BF16), b.astype(BF16), (((0,), (0,)), ((), ())),
                           preferred_element_type=F32)


def _split(x):
    hi = x.astype(BF16)
    lo = (x - hi.astype(F32)).astype(BF16)
    return hi, lo


def _sel_dot(x, sel):
    hi, lo = _split(x)
    return (jnp.dot(hi, sel, preferred_element_type=F32)
            + jnp.dot(lo, sel, preferred_element_type=F32))


def _sel_dot_left(sel, x):
    hi, lo = _split(x)
    return (jnp.dot(sel, hi, preferred_element_type=F32)
            + jnp.dot(sel, lo, preferred_element_type=F32))


def _iota(shape, dim):
    return lax.broadcasted_iota(jnp.int32, shape, dim)


def _block_ones(n, seg):
    same = (_iota((n, n), 0) // seg) == (_iota((n, n), 1) // seg)
    return jnp.where(same, 1.0, 0.0).astype(BF16)


def _seg_sum(x, seg):
    sel = _block_ones(LANES, seg)
    parts = [_sel_dot(x[:, j * LANES:(j + 1) * LANES], sel) for j in range(x.shape[1] // LANES)]
    return jnp.concatenate(parts, axis=1)


def _rwkv_proj_kernel(*refs, seq_tiles, has_vres, single_step):
    it = iter(refs)
    x_ref, shift_ref, gpre_ref, mu_ref, vec_ref = (next(it) for _ in range(5))
    wr_ref, wk_ref, wv_ref, w1_ref, w2_ref, a1_ref, a2_ref, g1_ref, g2_ref = (next(it) for _ in range(9))
    if has_vres:
        v1_ref, v2_ref, vfirst_ref = (next(it) for _ in range(3))
    r_out, w_out, k_out, v_out, kk_out, b_out, g_out, shift_out = (next(it) for _ in range(8))
    last_scr = next(it)

    h = _rms(x_ref[...], gpre_ref[...])
    tm = h.shape[0]
    if single_step:
        prev = shift_ref[...]
        shift_out[...] = h
    else:
        i = pl.program_id(0)
        b = i // seq_tiles
        ti = i % seq_tiles
        first = jnp.where(ti == 0, shift_ref[pl.ds(b, 1), :], last_scr[...])
        prev = jnp.where(_iota((tm, 1), 0) == 0, first, pltpu.roll(h, 1, 0))
        last_scr[...] = h[tm - 1:tm, :]

        @pl.when(ti == seq_tiles - 1)
        def _():
            shift_out[pl.ds(b, 1), :] = h[tm - 1:tm, :]

    d = prev - h
    mix = lambda n: h + d * mu_ref[n:n + 1, :]
    w0, a0, k_k, k_a = (vec_ref[n:n + 1, :] for n in range(4))

    xv = mix(3)
    r = _bdot(mix(0), wr_ref[...])
    k = _bdot(mix(2), wk_ref[...])
    v = _bdot(xv, wv_ref[...])
    wl = w0 + _bdot(jnp.tanh(_bdot(mix(1), w1_ref[...])), w2_ref[...])
    a = _sigmoid(a0 + _bdot(_bdot(mix(4), a1_ref[...]), a2_ref[...]))
    if has_vres:
        gate = _sigmoid(vec_ref[4:5, :] + _bdot(_bdot(xv, v1_ref[...]), v2_ref[...]))
        v = v + (vfirst_ref[...].astype(F32) - v) * gate
    g = _bdot(_sigmoid(_bdot(mix(5), g1_ref[...])), g2_ref[...])

    kk = k * k_k
    kk = kk / jnp.maximum(jnp.sqrt(_seg_sum(kk * kk, RW_HEAD)), 1e-12)
    put = lambda ref, val: ref.__setitem__(Ellipsis, val.astype(ref.dtype))
    put(r_out, r)
    w_out[...] = -RW_DECAY_SCALE * _sigmoid(wl)
    put(k_out, k * (1.0 + (a - 1.0) * k_a))
    put(v_out, v)
    put(kk_out, kk)
    put(b_out, kk * a)
    put(g_out, g)


def _rwkv_proj(x, shift0, gpre, mu, vecs, mats, vres, *, seq_len, tm, act_dtype=F32):
    M, D = x.shape
    B = M // seq_len
    single_step = seq_len == 1
    seq_tiles = 1 if single_step else seq_len // tm
    row = pl.BlockSpec((tm, D), lambda i: (i, 0))
    shift_spec = row if single_step else _const_spec((B, D))
    shift_out_spec = row if single_step else pl.BlockSpec((B, D), lambda i: (0, 0))
    ins = [x, shift0, gpre, mu, vecs] + list(mats)
    specs = [row, shift_spec, _const_spec(gpre.shape), _const_spec(mu.shape), _const_spec(vecs.shape)]
    specs += [_const_spec(m.shape) for m in mats]
    if vres is not None:
        v1, v2, vfirst = vres
        ins += [v1, v2, vfirst]
        specs += [_const_spec(v1.shape), _const_spec(v2.shape), row]
    act = jax.ShapeDtypeStruct((M, D), act_dtype)
    outs = [act, jax.ShapeDtypeStruct((M, D), F32)] + [act] * 5 + [jax.ShapeDtypeStruct((B, D), F32)]
    return pl.pallas_call(
        functools.partial(_rwkv_proj_kernel, seq_tiles=seq_tiles, has_vres=vres is not None,
                          single_step=single_step),
        grid=(M // tm,), in_specs=specs, out_specs=[row] * 7 + [shift_out_spec], out_shape=outs,
        scratch_shapes=[pltpu.VMEM((1, D), F32)],
        compiler_params=_params(("arbitrary",)), name="rwkv_proj")(*ins)


def _wkv_step_kernel(r_ref, lw_ref, k_ref, v_ref, kk_ref, b_ref, s0_ref, o_ref, sout_ref, *, pairs):
    n = RW_HEAD
    ones2 = _block_ones(LANES, n)
    diag2 = jnp.where(_iota((n, LANES), 0) == (_iota((n, LANES), 1) % n), 1.0, 0.0)
    for p in range(pairs):
        sl = slice(p * LANES, (p + 1) * LANES)
        s = jnp.concatenate([s0_ref[2 * p], s0_ref[2 * p + 1]], axis=1)
        lhs = jnp.concatenate([s * kk_ref[:, sl], diag2 * v_ref[:, sl]], axis=0)
        res = jnp.dot(lhs.astype(BF16), ones2, preferred_element_type=F32)
        s = s * jnp.exp(lw_ref[:, sl]) - res[:n] * b_ref[:, sl] + res[n:] * k_ref[:, sl]
        ob = jnp.dot((s * r_ref[:, sl]).astype(BF16), ones2, preferred_element_type=F32)
        o_ref[:, sl] = jnp.sum(ob * diag2, axis=0, keepdims=True)
        sout_ref[2 * p] = s[:, :n]
        sout_ref[2 * p + 1] = s[:, n:]


def _wkv_step(r, lw, k, v, kk, b, s0):
    B, _, D = r.shape
    H = s0.shape[1]
    seq = pl.BlockSpec((None, 1, D), lambda bi: (bi, 0, 0))
    st = pl.BlockSpec((None, H, RW_HEAD, RW_HEAD), lambda bi: (bi, 0, 0, 0))
    return pl.pallas_call(
        functools.partial(_wkv_step_kernel, pairs=H // 2), grid=(B,),
        in_specs=[seq] * 6 + [st], out_specs=[seq, st],
        out_shape=[jax.ShapeDtypeStruct((B, 1, D), F32), jax.ShapeDtypeStruct(s0.shape, F32)],
        compiler_params=_params(("arbitrary",)), name="wkv_step")(r, lw, k, v, kk, b, s0)


WKV_CHUNK = 64
WKV_GROUP = 256 // RW_HEAD


def _wkv_chunk_kernel(r_ref, lw_ref, k_ref, v_ref, kk_ref, b_ref, s0_ref, o_ref, sout_ref, s_scr, *, tblk, groups,
                      unroll):
    C, W = WKV_CHUNK, 256
    ti = pl.program_id(1)

    @pl.when(ti == 0)
    def _():
        s_scr[...] = s0_ref[...]

    blk = jnp.where((_iota((W, W), 0) // C) == (_iota((W, W), 1) // C), 1.0, 0.0)
    blkb = blk.astype(BF16)
    tcol, trow = _iota((C, W), 1) % C, _iota((C, W), 0)
    strict, incl = tcol < trow, tcol <= trow
    eye = jnp.where(tcol == trow, 1.0, 0.0)
    tri = jnp.where(_iota((C, C), 1) <= _iota((C, C), 0), 1.0, 0.0).astype(BF16)

    def bd(x):
        return jnp.concatenate([x.astype(BF16)] * (W // C), axis=0) * blkb

    def sbs(y):
        ym = y * blk
        return (ym[0:C] + ym[C:2 * C]) + (ym[2 * C:3 * C] + ym[3 * C:4 * C])

    mm = lambda a, b: jnp.dot(a.astype(BF16), b, preferred_element_type=F32)

    def chunk(c, carry):
        t0 = pl.multiple_of(c * (C * unroll), C * unroll)
        G = range(groups)
        units = [(u, g) for u in range(unroll) for g in G]
        each = lambda f, *cols: [f(*args) for args in zip(*cols)]
        load = lambda ref: [ref[pl.ds(t0 + u * C, C), g * W:(g + 1) * W].astype(F32) for u, g in units]
        r, lw, k, v, kk, b = (load(ref) for ref in (r_ref, lw_ref, k_ref, v_ref, kk_ref, b_ref))
        cum = each(lambda x: _sel_dot_left(tri, x), lw)
        tot = each(lambda x: x[C - 1:C, :], cum)
        e_neg = each(lambda x: jnp.exp(-x), cum)
        rt = each(lambda x, y: x * jnp.exp(y), r, cum)
        at = each(lambda x, y, z: x * jnp.exp(y - z), kk, cum, lw)
        lhs2 = each(lambda x, y: jnp.concatenate([x, y], axis=0).astype(BF16), at, rt)
        gb = each(lambda l, x, e: _dot_nt(l, bd(x * e)), lhs2, b, e_neg)
        gk = each(lambda l, x, e: _dot_nt(l, bd(x * e)), lhs2, k, e_neg)
        aab = each(lambda x: jnp.where(strict, x[:C], 0.0), gb)
        arb = each(lambda x: jnp.where(incl, x[C:], 0.0).astype(BF16), gb)
        a2k = each(lambda x: jnp.concatenate([jnp.where(strict, x[:C], 0.0), jnp.where(incl, x[C:], 0.0)], axis=0), gk)
        av = each(lambda x, y: mm(x, bd(y)), a2k, v)
        x = each(lambda n: eye - n, aab)
        pw = aab
        pbd = each(bd, pw)
        for _ in range(5):
            pw = each(mm, pw, pbd)
            pbd = each(bd, pw)
            x = each(lambda y, z: y + mm(y, z), x, pbd)
        xb = each(lambda y: y.astype(BF16), x)
        p = each(lambda y, z: mm(y, bd(z)), xb, at)
        q = each(lambda y, z: mm(y, bd(z[:C])), xb, av)
        rp = each(lambda y, a, z: y - mm(a, bd(z)), rt, arb, p)
        op = each(lambda y, a, z: y[C:] - mm(a, bd(z)), av, arb, q)
        e_end = each(lambda t, y: jnp.exp(t - y), tot, cum)
        kp = each(lambda y, e: y * e, k, e_end)
        bp = each(lambda y, e: y * e, b, e_end)
        m_s = each(lambda t, y, z: eye * jnp.exp(t) - sbs(_dot_tn(y, z)), tot, bp, p)
        n_s = each(lambda y, z, vv, qq: sbs(_dot_tn(jnp.concatenate([y, z], axis=0),
                                                    jnp.concatenate([vv, -qq], axis=0))), kp, bp, v, q)
        state = [s_scr[g] for g in G]
        for u in range(unroll):
            sbd = each(bd, state)
            for g in G:
                i = u * groups + g
                o_ref[pl.ds(t0 + u * C, C), g * W:(g + 1) * W] = mm(rp[i], sbd[g]) + op[i]
            state = [mm(m_s[u * groups + g], sbd[g]) + n_s[u * groups + g] for g in G]
        for g in G:
            s_scr[g] = state[g]
        return carry

    lax.fori_loop(0, tblk // (C * unroll), chunk, 0)

    @pl.when(ti == pl.num_programs(1) - 1)
    def _():
        sout_ref[...] = s_scr[...]


def _wkv_chunked(r, lw, k, v, kk, b, s0, *, tblk, unroll=2):
    B, T, D = r.shape
    H, N = s0.shape[1], s0.shape[2]
    G = WKV_GROUP
    pack = lambda s: jnp.transpose(s.reshape(B, H // G, G, N, N), (0, 1, 4, 2, 3)).reshape(B, H // G, N, G * N)
    unpack = lambda s: jnp.transpose(s.reshape(B, H // G, N, G, N), (0, 1, 3, 4, 2)).reshape(B, H, N, N)
    seq = pl.BlockSpec((None, tblk, D), lambda bi, ti: (bi, ti, 0))
    st = pl.BlockSpec((None, H // G, N, G * N), lambda bi, ti: (bi, 0, 0, 0))
    o, s_new = pl.pallas_call(
        functools.partial(_wkv_chunk_kernel, tblk=tblk, groups=H // G, unroll=unroll),
        grid=(B, T // tblk), in_specs=[seq] * 6 + [st], out_specs=[seq, st],
        out_shape=[jax.ShapeDtypeStruct((B, T, D), F32), jax.ShapeDtypeStruct((B, H // G, N, G * N), F32)],
        scratch_shapes=[pltpu.VMEM((H // G, N, G * N), F32)],
        compiler_params=_params(("arbitrary", "arbitrary")), name="wkv_chunked")(r, lw, k, v, kk, b, pack(s0))
    return o, unpack(s_new)


def _finish(y, wo_ref, x_ref, gpost_ref, out_ref):
    z = jnp.dot(y.astype(BF16), wo_ref[...], preferred_element_type=F32)
    out_ref[...] = x_ref[...] + _rms(z, gpost_ref[...])


def _rwkv_out_kernel(o_ref, r_ref, k_ref, v_ref, g_ref, x_ref, vec_ref, wo_ref, gpost_ref, out_ref):
    rk, lnw, lnb = (vec_ref[n:n + 1, :] for n in range(3))
    o = o_ref[...]
    dev = o - _seg_sum(o, RW_HEAD) * (1.0 / RW_HEAD)
    var = _seg_sum(dev * dev, RW_HEAD) * (1.0 / RW_HEAD)
    on = dev * lax.rsqrt(var + RW_GN_EPS) * lnw + lnb
    f32 = lambda ref: ref[...].astype(F32)
    bonus = _seg_sum(f32(r_ref) * f32(k_ref) * rk, RW_HEAD) * f32(v_ref)
    _finish((on + bonus) * f32(g_ref), wo_ref, x_ref, gpost_ref, out_ref)


def _hgrn_out_kernel(o_ref, g_ref, x_ref, ng_ref, wo_ref, gpost_ref, out_ref):
    o = o_ref[...]
    parts = []
    for j in range(o.shape[1] // HG_HEAD):
        oj = o[:, j * HG_HEAD:(j + 1) * HG_HEAD]
        parts.append(oj * lax.rsqrt(jnp.mean(oj * oj, axis=-1, keepdims=True) + NORM_EPS))
    g = g_ref[...].astype(F32)
    y = jnp.concatenate(parts, axis=1) * ng_ref[...] * (g * _sigmoid(g))
    _finish(y, wo_ref, x_ref, gpost_ref, out_ref)


def _fox_out_kernel(o_ref, g_ref, x_ref, wo_ref, gpost_ref, out_ref):
    _finish(o_ref[...] * _sigmoid(g_ref[...].astype(F32)), wo_ref, x_ref, gpost_ref, out_ref)


def _mixer_out(body, rows, consts, *, tm, name):
    M, D = rows[0].shape
    row = pl.BlockSpec((tm, D), lambda i: (i, 0))
    return pl.pallas_call(
        body, grid=(M // tm,), in_specs=[row] * len(rows) + [_const_spec(c.shape) for c in consts],
        out_specs=row, out_shape=jax.ShapeDtypeStruct((M, D), F32),
        compiler_params=_params(("arbitrary",)), name=name)(*rows, *consts)


def _ffn_kernel(x_ref, gpre_ref, gpost_ref, wg_ref, wu_ref, wd_ref, out_ref):
    x = x_ref[...]
    h = _rms(x, gpre_ref[...]).astype(BF16)
    a = jnp.dot(h, wg_ref[...], preferred_element_type=F32)
    u = jnp.dot(h, wu_ref[...], preferred_element_type=F32)
    act = (a * _sigmoid(a) * u).astype(BF16)
    z = jnp.dot(act, wd_ref[...], preferred_element_type=F32)
    out_ref[...] = x + _rms(z, gpost_ref[...])


def _ffn(x, gpre, gpost, wg, wu, wd, *, tm):
    M, D = x.shape
    row = pl.BlockSpec((tm, D), lambda i: (i, 0))
    consts = [gpre, gpost, wg, wu, wd]
    return pl.pallas_call(
        _ffn_kernel, grid=(M // tm,), in_specs=[row] + [_const_spec(c.shape) for c in consts],
        out_specs=row, out_shape=jax.ShapeDtypeStruct((M, D), F32),
        compiler_params=_params(("arbitrary",)), name="ffn")(x, *consts)


def _hgrn_proj_kernel(x_ref, gpre_ref, lb_ref, w_ref, q_out, k_out, v_out, lf_out, g_out, *, layer):
    D = x_ref.shape[1]
    h = _rms(x_ref[...], gpre_ref[...]).astype(BF16)
    proj = lambda n: jnp.dot(h, w_ref[:, n * D:(n + 1) * D], preferred_element_type=F32)
    lbw = lb_ref[...]
    e = jnp.exp(lbw - jnp.max(lbw, axis=0, keepdims=True))
    lb = jnp.sum(e[1:layer + 1], axis=0, keepdims=True) / jnp.sum(e, axis=0, keepdims=True)
    q = proj(0)
    f = lb + (1.0 - lb) * _sigmoid(proj(1))
    q_out[...] = (q * _sigmoid(q)).astype(q_out.dtype)
    k_out[...] = (1.0 - f).astype(k_out.dtype)
    lf_out[...] = jnp.log(f)
    v_out[...] = proj(2).astype(v_out.dtype)
    g_out[...] = proj(3).astype(g_out.dtype)


def _hgrn_proj(x, gpre, hg_lb, w_in, *, layer, tm, act_dtype=F32):
    M, D = x.shape
    row = pl.BlockSpec((tm, D), lambda i: (i, 0))
    return pl.pallas_call(
        functools.partial(_hgrn_proj_kernel, layer=layer), grid=(M // tm,),
        in_specs=[row, _const_spec(gpre.shape), _const_spec(hg_lb.shape), _const_spec(w_in.shape)],
        out_specs=[row] * 5,
        out_shape=[jax.ShapeDtypeStruct((M, D), dt) for dt in (act_dtype, act_dtype, act_dtype, F32, act_dtype)],
        compiler_params=_params(("arbitrary",)), name="hgrn_proj")(x, gpre, hg_lb, w_in)


def _gla_kernel(q_ref, k_ref, v_ref, lf_ref, s0_ref, o_ref, sout_ref, s_scr, *, tblk, heads, unroll):
    ti = pl.program_id(1)
    C = HG_CHUNK

    @pl.when(ti == 0)
    def _():
        s_scr[...] = s0_ref[...]

    lower = _iota((C, C), 0) >= _iota((C, C), 1)
    tri = jnp.where(lower, 1.0, 0.0).astype(BF16)

    def chunk(c, carry):
        t0 = pl.multiple_of(c * (C * unroll), C * unroll)
        hs = range(heads)
        units = [(u, hd) for u in range(unroll) for hd in hs]
        each = lambda f, *cols: [f(*args) for args in zip(*cols)]
        load = lambda ref: [ref[pl.ds(t0 + u * C, C), hd * HG_HEAD:(hd + 1) * HG_HEAD].astype(F32)
                            for u, hd in units]
        q, k, v, lf = load(q_ref), load(k_ref), load(v_ref), load(lf_ref)
        b = each(lambda x: _sel_dot_left(tri, x), lf)
        qd = each(lambda x, y: (x * jnp.exp(y)).astype(BF16), q, b)
        att = each(lambda x, y, z: jnp.where(lower, _dot_nt(x, y * jnp.exp(-z)), 0.0), qd, k, b)
        o2 = each(_bdot, att, v)
        kv = each(lambda x, y, z: _dot_tn(x, y * jnp.exp(z[C - 1:C, :] - z)), v, k, b)
        decay = each(lambda z: jnp.exp(z[C - 1:C, :]), b)
        st = [s_scr[hd] for hd in hs]
        for u in range(unroll):
            o1 = [_dot_nt(qd[u * heads + hd], st[hd]) for hd in hs]
            for hd in hs:
                i = u * heads + hd
                o_ref[pl.ds(t0 + u * C, C), hd * HG_HEAD:(hd + 1) * HG_HEAD] = o1[hd] + o2[i]
            st = [st[hd] * decay[u * heads + hd] + kv[u * heads + hd] for hd in hs]
        for hd in hs:
            s_scr[hd] = st[hd]
        return carry

    lax.fori_loop(0, tblk // (C * unroll), chunk, 0)

    @pl.when(ti == pl.num_programs(1) - 1)
    def _():
        sout_ref[...] = s_scr[...]


def _gla(q, k, v, lf, s0t, *, tblk, unroll=8):
    B, T, D = q.shape
    H = s0t.shape[1]
    seq = pl.BlockSpec((None, tblk, D), lambda bi, ti: (bi, ti, 0))
    st = pl.BlockSpec((None, H, HG_HEAD, HG_HEAD), lambda bi, ti: (bi, 0, 0, 0))
    return pl.pallas_call(
        functools.partial(_gla_kernel, tblk=tblk, heads=H, unroll=unroll),
        grid=(B, T // tblk), in_specs=[seq] * 4 + [st], out_specs=[seq, st],
        out_shape=[jax.ShapeDtypeStruct((B, T, D), F32), jax.ShapeDtypeStruct(s0t.shape, F32)],
        scratch_shapes=[pltpu.VMEM((H, HG_HEAD, HG_HEAD), F32)],
        compiler_params=_params(("arbitrary", "arbitrary")), name="gla_chunked")(q, k, v, lf, s0t)


def _gla_step_kernel(q_ref, k_ref, v_ref, lf_ref, s0_ref, o_ref, sout_ref, *, heads):
    n = HG_HEAD
    eye = jnp.where(_iota((n, n), 0) == _iota((n, n), 1), 1.0, 0.0)
    ones = jnp.ones((n, n), BF16)
    for hd in range(heads):
        sl = slice(hd * n, (hd + 1) * n)
        vcol = _sel_dot(eye * v_ref[:, sl], ones)
        st = s0_ref[hd] * jnp.exp(lf_ref[:, sl]) + vcol * k_ref[:, sl]
        sout_ref[hd] = st
        o_ref[:, sl] = _dot_nt(jnp.broadcast_to(q_ref[:, sl], (8, n)), st)[0:1, :]


def _gla_step(q, k, v, lf, s0t):
    B, _, D = q.shape
    H = s0t.shape[1]
    seq = pl.BlockSpec((None, 1, D), lambda bi: (bi, 0, 0))
    st = pl.BlockSpec((None, H, HG_HEAD, HG_HEAD), lambda bi: (bi, 0, 0, 0))
    return pl.pallas_call(
        functools.partial(_gla_step_kernel, heads=H), grid=(B,),
        in_specs=[seq] * 4 + [st], out_specs=[seq, st],
        out_shape=[jax.ShapeDtypeStruct((B, 1, D), F32), jax.ShapeDtypeStruct(s0t.shape, F32)],
        compiler_params=_params(("arbitrary",)), name="gla_step")(q, k, v, lf, s0t)


def _fox_proj_kernel(x_ref, gpre_ref, w_ref, wf_ref, bf_ref, qn_ref, kn_ref,
                     q_out, k_out, v_out, g_out, lf_out, c_out, *rest, seq_tiles, single_step):
    carry_scr = rest[-1]
    D = x_ref.shape[1]
    tm = x_ref.shape[0]
    h = _rms(x_ref[...], gpre_ref[...]).astype(BF16)
    proj = lambda n: jnp.dot(h, w_ref[:, n * D:(n + 1) * D], preferred_element_type=F32)
    head_norm = lambda t, gain: t * lax.rsqrt(_seg_sum(t * t, FX_HEAD) * (1.0 / FX_HEAD) + NORM_EPS) * gain
    q_out[...] = (head_norm(proj(0), qn_ref[...]) * (FX_HEAD ** -0.5)).astype(q_out.dtype)
    k = head_norm(proj(1), kn_ref[...])
    v = proj(2)
    k_out[...] = k
    v_out[...] = v
    if not single_step:
        kt_out, vt_out = rest[0], rest[1]
        kt_out[...] = k.T
        vt_out[...] = v.T
    g_out[...] = proj(3).astype(g_out.dtype)
    z = jnp.dot(h, wf_ref[...], preferred_element_type=F32) + bf_ref[...]
    lf = jnp.minimum(z, 0.0) - jnp.log1p(jnp.exp(-jnp.abs(z)))
    lf_out[...] = lf[:, :FX_HEADS]
    if single_step:
        c_out[...] = lf[:, :FX_HEADS]
    else:
        ti = pl.program_id(0) % seq_tiles

        @pl.when(ti == 0)
        def _():
            carry_scr[...] = jnp.zeros_like(carry_scr)

        tri = jnp.where(_iota((tm, tm), 0) >= _iota((tm, tm), 1), 1.0, 0.0).astype(BF16)
        c = _sel_dot_left(tri, lf) + carry_scr[...]
        carry_scr[...] = c[tm - 1:tm, :]
        c_out[...] = c[:, :FX_HEADS]


def _fox_proj(x, gpre, w_main, w_f, b_f, qn, kn, *, seq_len, tm, act_dtype=F32):
    M, D = x.shape
    single_step = seq_len == 1
    row = pl.BlockSpec((tm, D), lambda i: (i, 0))
    hrow = pl.BlockSpec((tm, FX_HEADS), lambda i: (i, 0))
    consts = [gpre, w_main, w_f, b_f, qn, kn]
    seq_tiles = 1 if single_step else seq_len // tm
    out_specs = [row] * 4 + [hrow] * 2
    out_shape = ([jax.ShapeDtypeStruct((M, D), dt) for dt in (act_dtype, F32, F32, act_dtype)]
                 + [jax.ShapeDtypeStruct((M, FX_HEADS), F32)] * 2)
    if not single_step:
        tspec = pl.BlockSpec((None, D, tm), lambda i: (i // seq_tiles, 0, i % seq_tiles))
        out_specs += [tspec] * 2
        out_shape += [jax.ShapeDtypeStruct((M // seq_len, D, seq_len), F32)] * 2
    return pl.pallas_call(
        functools.partial(_fox_proj_kernel, seq_tiles=seq_tiles, single_step=single_step),
        grid=(M // tm,), in_specs=[row] + [_const_spec(c.shape) for c in consts],
        out_specs=out_specs, out_shape=out_shape,
        scratch_shapes=[pltpu.VMEM((1, LANES), F32)],
        compiler_params=_params(("arbitrary",)), name="fox_proj")(x, *consts)


def _fox_attn_kernel(q_ref, k_ref, vt_ref, cq_ref, ck_ref, o_ref, *, tq, tk, pairs):
    qi = pl.program_id(2)
    n = FX_HEAD
    heads = range(2 * pairs)
    first = _iota((1, LANES), 1) < n
    qh, cqh = [], []
    for p in range(pairs):
        q = q_ref[:, p * LANES:(p + 1) * LANES].astype(F32) * LOG2E
        qh += [jnp.where(first, q, 0.0).astype(BF16), jnp.where(first, 0.0, q).astype(BF16)]
        cq = cq_ref[p] * LOG2E
        cqh += [cq[0:1, :], cq[1:2, :]]
    full_blocks = (qi * tq) // tk

    def block(ki, carry, diagonal):
        k0 = pl.multiple_of(ki * tk, tk)
        kb = [k_ref[pl.ds(k0, tk), p * LANES:(p + 1) * LANES].astype(BF16) for p in range(pairs)]
        keep = None
        if diagonal:
            keep = (_iota((tk, tq), 0) - _iota((tk, tq), 1)) <= (qi * tq - k0)
        s = [_dot_nt(kb[h // 2], qh[h]) for h in heads]
        out = []
        probs = []
        for h in heads:
            m, l, _ = carry[h]
            ck = ck_ref[h // 2, pl.ds(k0, tk), :] * LOG2E
            sh = s[h] + cqh[h] - ck[:, h % 2:h % 2 + 1]
            if diagonal:
                sh = jnp.where(keep, sh, NEG_BIG)
            m_new = jnp.maximum(m, jnp.max(sh, axis=0, keepdims=True))
            p = jnp.exp2(sh - m_new)
            a = jnp.exp2(m - m_new)
            probs.append((m_new, l * a + jnp.sum(p, axis=0, keepdims=True), a, p.astype(BF16)))
        for h in heads:
            m_new, l_new, a, p = probs[h]
            vt = vt_ref[h * n:(h + 1) * n, pl.ds(k0, tk)].astype(BF16)
            out.append((m_new, l_new, carry[h][2] * a + jnp.dot(vt, p, preferred_element_type=F32)))
        return tuple(out)

    row = lambda val: jnp.full((1, tq), val, F32)
    init = tuple((row(NEG_BIG), row(0.0), jnp.zeros((n, tq), F32)) for _ in heads)
    carry = lax.fori_loop(0, full_blocks, lambda ki, c: block(ki, c, False), init)
    fin = block(full_blocks, carry, True)
    o_ref[...] = jnp.concatenate([acc / l for _, l, acc in fin], axis=0).T


def _fox_attn(q, k, vt, c, *, tq, tk, pairs):
    B, T, D = q.shape
    hp = D // LANES
    W = pairs * LANES
    ck = jnp.transpose(c.reshape(B, T, hp, 2), (0, 2, 1, 3))
    cq = jnp.transpose(c.reshape(B, T, hp, 2), (0, 2, 3, 1))
    qs = pl.BlockSpec((None, tq, W), lambda b, p, i: (b, i, p))
    ks = pl.BlockSpec((None, T, W), lambda b, p, i: (b, 0, p))
    vs = pl.BlockSpec((None, W, T), lambda b, p, i: (b, p, 0))
    cqs = pl.BlockSpec((None, pairs, 2, tq), lambda b, p, i: (b, p, 0, i))
    cks = pl.BlockSpec((None, pairs, T, 2), lambda b, p, i: (b, p, 0, 0))
    return pl.pallas_call(
        functools.partial(_fox_attn_kernel, tq=tq, tk=tk, pairs=pairs), grid=(B, hp // pairs, T // tq),
        in_specs=[qs, ks, vs, cqs, cks], out_specs=qs, out_shape=jax.ShapeDtypeStruct((B, T, D), F32),
        compiler_params=_params(("arbitrary",) * 3), name="fox_attn")(q, k, vt, cq, ck)


def _fox_paged_kernel(*refs, npg):
    it = iter(refs)
    _pt_ref = next(it)
    kt_refs, vt_refs, lf_refs = ([next(it) for _ in range(npg)] for _ in range(3))
    q_ref, kn_ref, vn_ref, cq_ref, o_ref, m_scr, l_scr, carry_scr, acc_scr = (next(it) for _ in range(9))
    g = pl.program_id(1)
    H, N, P = FX_HEADS, FX_HEAD, PAGE_SIZE
    D = H * N

    @pl.when(g == 0)
    def _():
        m_scr[...] = jnp.full_like(m_scr, NEG_BIG)
        l_scr[...] = jnp.zeros_like(l_scr)
        carry_scr[...] = jnp.zeros_like(carry_scr)
        acc_scr[...] = jnp.zeros_like(acc_scr)

    own = _iota((H, D), 0) == _iota((H, D), 1) // N
    qbd = jnp.where(own, q_ref[...], 0.0).astype(BF16)
    later = jnp.where(_iota((P, P), 0) > _iota((P, P), 1), 1.0, 0.0).astype(BF16)
    cq = cq_ref[...]
    carry = carry_scr[...]
    scores = [None] * npg
    for j in reversed(range(npg)):
        lf = lf_refs[j][...]
        s = jnp.dot(qbd, kt_refs[j][...].astype(BF16), preferred_element_type=F32)
        scores[j] = s + _sel_dot(lf, later) + (carry + cq)
        carry = carry + jnp.sum(lf, axis=1, keepdims=True)
    carry_scr[...] = carry
    m_old = m_scr[...]
    m_new = m_old
    for j in range(npg):
        m_new = jnp.maximum(m_new, jnp.max(scores[j], axis=1, keepdims=True))
    alpha = jnp.exp(m_old - m_new)
    l_new = l_scr[...] * alpha
    acc = acc_scr[...] * alpha
    for j in range(npg):
        p = jnp.exp(scores[j] - m_new)
        l_new = l_new + jnp.sum(p, axis=1, keepdims=True)
        acc = acc + _dot_nt(p, vt_refs[j][...])
    acc_scr[...] = acc
    m_scr[...] = m_new
    l_scr[...] = l_new

    @pl.when(g == pl.num_programs(1) - 1)
    def _():
        s_new = jnp.sum(jnp.where(own, q_ref[...] * kn_ref[...], 0.0), axis=1, keepdims=True)
        m_fin = jnp.maximum(m_new, s_new)
        a_fin = jnp.exp(m_new - m_fin)
        p_new = jnp.exp(s_new - m_fin)
        mix = (acc * a_fin + vn_ref[...] * p_new) / (l_new * a_fin + p_new)
        o_ref[...] = jnp.sum(jnp.where(own, mix, 0.0), axis=0, keepdims=True)


def _fox_paged(q, k_new, v_new, lf_new, cache_kt, cache_vt, cache_lft, page_table, *, npg):
    B, _, D = q.shape
    H, P = FX_HEADS, PAGE_SIZE
    n_pages = page_table.shape[1]
    groups = n_pages // npg

    def page(j):
        return lambda b, g, pt: (pt[b, (groups - 1 - g) * npg + j], 0, 0)

    kspecs = [pl.BlockSpec((None, D, P), page(j)) for j in range(npg)]
    lspecs = [pl.BlockSpec((None, H, P), page(j)) for j in range(npg)]
    tok = pl.BlockSpec((None, 1, D), lambda b, g, pt: (b, 0, 0))
    tokh = pl.BlockSpec((None, H, 1), lambda b, g, pt: (b, 0, 0))
    grid_spec = pltpu.PrefetchScalarGridSpec(
        num_scalar_prefetch=1, grid=(B, groups),
        in_specs=kspecs + kspecs + lspecs + [tok, tok, tok, tokh], out_specs=tok,
        scratch_shapes=[pltpu.VMEM((H, 1), F32), pltpu.VMEM((H, 1), F32), pltpu.VMEM((H, 1), F32),
                        pltpu.VMEM((H, D), F32)])
    return pl.pallas_call(
        functools.partial(_fox_paged_kernel, npg=npg), grid_spec=grid_spec,
        out_shape=jax.ShapeDtypeStruct((B, 1, D), F32),
        compiler_params=_params(("arbitrary", "arbitrary")), name="fox_paged")(
            page_table, *([cache_kt] * npg), *([cache_vt] * npg), *([cache_lft] * npg),
            q, k_new, v_new, lf_new.reshape(B, H, 1))


def _row_tile(m, want):
    return want if m % want == 0 else m


def kernel(x_prompt, x_sample, state_rwkv_wkv, state_rwkv_shift, state_hgrn, cache_fox_k, cache_fox_v, cache_fox_logf, page_table, norm_pre_mix, norm_post_mix, norm_pre_ffn, norm_post_ffn, rw_mu, rw_wr, rw_wk, rw_wv, rw_wo, rw_w0, rw_w1, rw_w2, rw_a0, rw_a1, rw_a2, rw_v0, rw_v1, rw_v2, rw_g1, rw_g2, rw_kk, rw_ka, rw_rk, rw_lnw, rw_lnb, hg_w_in, hg_lb, hg_norm, hg_wo, fx_w_in, fx_bf, fx_qn, fx_kn, fx_wo, ffn_wg, ffn_wu, ffn_wd):
    D = D_MODEL
    bf = lambda t: t.astype(BF16)
    vec = lambda t: t.reshape(1, -1).astype(F32)

    n_cache_pages = cache_fox_k.shape[1]
    cache_k = jnp.transpose(cache_fox_k, (0, 1, 3, 4, 2)).reshape(-1, D, PAGE_SIZE)
    cache_v = jnp.transpose(cache_fox_v, (0, 1, 3, 4, 2)).reshape(-1, D, PAGE_SIZE)
    cache_lf = jnp.transpose(cache_fox_logf, (0, 1, 3, 2)).reshape(-1, FX_HEADS, PAGE_SIZE)

    def trunk(x3, wkv0, shift0, hg0, paged):
        B, T, _ = x3.shape
        M = B * T
        x = x3.reshape(M, D)
        tm = _row_tile(M, 256)
        act = F32 if T == 1 else BF16
        v_first = None
        wkv_n, shift_n, hg_n, k_n, v_n, lf_n = [], [], [], [], [], []
        for i in range(DEPTH):
            j = i // N_MIXERS
            gpre, gpost = vec(norm_pre_mix[i]), vec(norm_post_mix[i])
            if i % N_MIXERS == 0:
                vecs = [rw_w0[j], rw_a0[j], rw_kk[j], rw_ka[j]]
                vecs += [rw_v0[j - 1]] if j > 0 else []
                vecs = jnp.stack(vecs + [jnp.zeros((D,), F32)] * (8 - len(vecs)))
                mats = [bf(t[j]) for t in (rw_wr, rw_wk, rw_wv, rw_w1, rw_w2, rw_a1, rw_a2, rw_g1, rw_g2)]
                vres = None if j == 0 else (bf(rw_v1[j - 1]), bf(rw_v2[j - 1]), v_first)
                r, w, k, v, kk, kb, g, shift = _rwkv_proj(x, shift0[j], gpre, rw_mu[j], vecs, mats, vres,
                                                          seq_len=T, tm=tm, act_dtype=act)
                if j == 0:
                    v_first = v
                seq = lambda t: t.reshape(B, T, D)
                wkv = _wkv_step if T == 1 else functools.partial(_wkv_chunked, tblk=min(T, 256))
                o, s_new = wkv(seq(r), seq(w), seq(k), seq(v), seq(kk), seq(kb), wkv0[j])
                ovec = jnp.stack([rw_rk[j].reshape(-1), rw_lnw[j], rw_lnb[j]] + [jnp.zeros((D,), F32)] * 5)
                x = _mixer_out(_rwkv_out_kernel, [o.reshape(M, D), r, k, v, g, x],
                               [ovec, bf(rw_wo[j]), gpost], tm=tm, name="rwkv_out")
                wkv_n.append(s_new)
                shift_n.append(shift)
            elif i % N_MIXERS == 1:
                q, k, v, lf, g = _hgrn_proj(x, gpre, hg_lb, bf(hg_w_in[j]), layer=i, tm=tm, act_dtype=act)
                seq = lambda t: t.reshape(B, T, D)
                s0t = jnp.swapaxes(hg0[j], -1, -2)
                if T == 1:
                    o, st = _gla_step(seq(q), seq(k), seq(v), seq(lf), s0t)
                else:
                    o, st = _gla(seq(q), seq(k), seq(v), seq(lf), s0t, tblk=min(T, 256))
                x = _mixer_out(_hgrn_out_kernel, [o.reshape(M, D), g, x],
                               [vec(hg_norm[j]), bf(hg_wo[j]), gpost], tm=tm, name="hgrn_out")
                hg_n.append(jnp.swapaxes(st, -1, -2))
            else:
                w_in = fx_w_in[j]
                w_f = jnp.pad(w_in[:, 4 * D:], ((0, 0), (0, LANES - FX_HEADS)))
                b_f = jnp.pad(fx_bf[j], (0, LANES - FX_HEADS)).reshape(1, LANES)
                tile = lambda t: jnp.tile(t, FX_HEADS).reshape(1, D)
                q, k, v, g, lf, c, *kvt = _fox_proj(x, gpre, bf(w_in[:, :4 * D]), bf(w_f), b_f, tile(fx_qn[j]),
                                                    tile(fx_kn[j]), seq_len=T, tm=tm, act_dtype=act)
                seq = lambda t: t.reshape(B, T, D)
                if paged:
                    pt = page_table + j * n_cache_pages
                    att = _fox_paged(seq(q), seq(k), seq(v), lf.reshape(B, 1, FX_HEADS),
                                     cache_k, cache_v, cache_lf, pt, npg=16)
                    k_n.append(k.reshape(B, T, FX_HEADS, FX_HEAD))
                    v_n.append(v.reshape(B, T, FX_HEADS, FX_HEAD))
                else:
                    kt, vt = kvt
                    att = _fox_attn(seq(q), seq(k), vt, c.reshape(B, T, FX_HEADS), tq=512, tk=512, pairs=2)
                    heads_last = lambda t: jnp.transpose(t.reshape(B, FX_HEADS, FX_HEAD, T), (0, 3, 1, 2))
                    k_n.append(heads_last(kt))
                    v_n.append(heads_last(vt))
                x = _mixer_out(_fox_out_kernel, [att.reshape(M, D), g, x], [bf(fx_wo[j]), gpost],
                               tm=tm, name="fox_out")
                lf_n.append(lf.reshape(B, T, FX_HEADS))
            x = _ffn(x, vec(norm_pre_ffn[i]), vec(norm_post_ffn[i]), bf(ffn_wg[i]), bf(ffn_wu[i]),
                     bf(ffn_wd[i]), tm=tm)
        return (x.reshape(B, T, D), jnp.stack(wkv_n), jnp.stack(shift_n), jnp.stack(hg_n),
                jnp.stack(k_n), jnp.stack(v_n), jnp.stack(lf_n))

    bp = x_prompt.shape[0]
    n_a, n_b = state_rwkv_wkv.shape[0], state_hgrn.shape[0]
    y_p, wkv_p, shift_p, hg_p, k_p, v_p, lf_p = trunk(
        x_prompt, jnp.zeros((n_a, bp) + state_rwkv_wkv.shape[2:], F32), jnp.zeros((n_a, bp, D), F32),
        jnp.zeros((n_b, bp) + state_hgrn.shape[2:], F32), False)
    y_s, wkv_s, shift_s, hg_s, k_s, v_s, lf_s = trunk(
        x_sample, state_rwkv_wkv, state_rwkv_shift, state_hgrn, True)
    return (y_p, y_s, wkv_p, wkv_s, shift_p, shift_s, hg_p, hg_s, k_p, k_s, v_p, v_s, lf_p, lf_s)
```
